```python
import jax, jax.numpy as jnp
from jax import lax
import numpy as np

D_MODEL = 1024
BATCH = 4
SEQ = 4096
DEPTH = 4

GRID_W = 64
CTX_LEN = 256
N_MIXERS = 2
N_A = (DEPTH + 1) // 2
N_B = DEPTH // 2
MLA_HEADS = 8
QK_NOPE = 128
QK_ROPE = 64
V_DIM = 128
QK_DIM = QK_NOPE + QK_ROPE
Q_LORA = 384
KV_LORA = 256
ROPE_THETA = 10000.0
Q_BLOCK = 128
GM_CHUNK = 128
GM_GROUPS = 8
GM_HALF = 2 * D_MODEL
GM_GROUP_DIM = GM_HALF // GM_GROUPS
FFN_HIDDEN = 4 * D_MODEL
N_MOD = 6
EPS = 1e-6

kernel_name = "hybrid_mla_gmlp_prefix_dit"


def rmsnorm(x, g):
    xf = x.astype(jnp.float32)
    y = xf * lax.rsqrt(jnp.mean(xf * xf, axis=-1, keepdims=True) + EPS)
    return (y * g.astype(jnp.float32)).astype(x.dtype)


def layernorm(x, g, b):
    xf = x.astype(jnp.float32)
    mu = jnp.mean(xf, axis=-1, keepdims=True)
    var = jnp.mean(jnp.square(xf - mu), axis=-1, keepdims=True)
    y = (xf - mu) * lax.rsqrt(var + EPS)
    return (y * g.astype(jnp.float32) + b.astype(jnp.float32)).astype(x.dtype)


def modulate(x, g, shift, scale):
    return rmsnorm(x, g) * (1.0 + scale) + shift


def axial_rope_tables(n_tokens, dtype):
    rows = n_tokens // GRID_W
    row = jnp.repeat(jnp.arange(rows, dtype=jnp.float32), GRID_W)
    col = jnp.tile(jnp.arange(GRID_W, dtype=jnp.float32), rows)
    half = QK_ROPE // 2
    inv = ROPE_THETA ** (-jnp.arange(0, half, 2, dtype=jnp.float32) / half)
    ang_r = row[:, None] * inv[None, :]
    ang_c = col[:, None] * inv[None, :]
    ang = jnp.concatenate([ang_r, ang_r, ang_c, ang_c], axis=-1)
    return jnp.cos(ang).astype(dtype), jnp.sin(ang).astype(dtype)


def rotate_axial(x):
    a1, a2, b1, b2 = jnp.split(x, 4, axis=-1)
    return jnp.concatenate([-a2, a1, -b2, b1], axis=-1)


def rope_part(t, cos, sin):
    nope, pe = t[..., :QK_NOPE], t[..., QK_NOPE:]
    c = cos[None, :, None, :]
    s = sin[None, :, None, :]
    return jnp.concatenate([nope, pe * c + rotate_axial(pe) * s], axis=-1)


def block_attention(q, k, v, scale):
    B, Sq, H, Dk = q.shape
    nb = Sq // Q_BLOCK
    qb = q.reshape(B, nb, Q_BLOCK, H, Dk).transpose(1, 0, 2, 3, 4)

    def one(qi):
        s = jnp.einsum('bqhd,bkhd->bhqk', qi, k, preferred_element_type=jnp.float32) * scale
        p = jax.nn.softmax(s, axis=-1)
        return jnp.einsum('bhqk,bkhd->bqhd', p.astype(v.dtype), v)

    o = lax.map(one, qb)
    return o.transpose(1, 0, 2, 3, 4).reshape(B, Sq, H, v.shape[-1])


def mla_queries(h, wq_a, q_a_norm, wq_b, q_norm):
    B, S, _ = h.shape
    cq = rmsnorm(h @ wq_a, q_a_norm)
    q = (cq @ wq_b).reshape(B, S, MLA_HEADS, QK_DIM)
    return rmsnorm(q, q_norm)


def mla_keys_values(h, wkv_a, kv_a_norm, wkv_b, k_norm):
    B, S, _ = h.shape
    kv_a = h @ wkv_a
    ckv, k_pe = kv_a[..., :KV_LORA], kv_a[..., KV_LORA:]
    kv = (rmsnorm(ckv, kv_a_norm) @ wkv_b).reshape(B, S, MLA_HEADS, QK_NOPE + V_DIM)
    k_nope, v = kv[..., :QK_NOPE], kv[..., QK_NOPE:]
    k_pe = jnp.broadcast_to(k_pe[:, :, None, :], (B, S, MLA_HEADS, QK_ROPE))
    k = rmsnorm(jnp.concatenate([k_nope, k_pe], axis=-1), k_norm)
    return k, v


def mla_mixer(h_lat, h_ctx, cos, sin, wq_a, q_a_norm, wq_b, wkv_a, kv_a_norm, wkv_b,
              q_norm, k_norm, wo, with_ctx):
    B, S, _ = h_lat.shape
    scale = QK_DIM ** -0.5
    k_lat, v_lat = mla_keys_values(h_lat, wkv_a, kv_a_norm, wkv_b, k_norm)
    k_ctx, v_ctx = mla_keys_values(h_ctx, wkv_a, kv_a_norm, wkv_b, k_norm)
    k_lat = rope_part(k_lat, cos, sin)
    q_lat = rope_part(mla_queries(h_lat, wq_a, q_a_norm, wq_b, q_norm), cos, sin)
    k_all = jnp.concatenate([k_ctx, k_lat], axis=1)
    v_all = jnp.concatenate([v_ctx, v_lat], axis=1)
    o_lat = block_attention(q_lat, k_all, v_all, scale).reshape(B, S, MLA_HEADS * V_DIM) @ wo
    o_ctx = None
    if with_ctx:
        q_ctx = mla_queries(h_ctx, wq_a, q_a_norm, wq_b, q_norm)
        o_ctx = block_attention(q_ctx, k_ctx, v_ctx, scale).reshape(
            B, h_ctx.shape[1], MLA_HEADS * V_DIM) @ wo
    return o_lat, o_ctx


def gmlp_mixer(h, w_in, ln_g, ln_b, ws, bs, w_out):
    B, S, _ = h.shape
    z = jax.nn.gelu(h @ w_in, approximate=False)
    u, v = z[..., :GM_HALF], z[..., GM_HALF:]
    v = layernorm(v, ln_g, ln_b)
    vc = v.reshape(B, S // GM_CHUNK, GM_CHUNK, GM_GROUPS, GM_GROUP_DIM)
    mixed = jnp.einsum('gpq,bnqgc->bnpgc', ws, vc) + bs.T[None, None, :, :, None]
    return (u * mixed.reshape(B, S, GM_HALF)) @ w_out


def squared_relu_mlp(h, w1, w2):
    return jnp.square(jax.nn.relu(h @ w1)) @ w2


def setup_inputs(seed: int = 0) -> dict:
    key = jax.random.key(seed)
    ks = jax.random.split(key, 32)
    D = D_MODEL

    def nrm(k, shape, scale):
        return jax.random.normal(k, shape, jnp.float32) * scale

    def gain(k, shape):
        return 1.0 + 0.05 * jax.random.normal(k, shape, jnp.float32)

    return {
        "x": nrm(ks[0], (BATCH, SEQ, D), 1.0),
        "c": nrm(ks[1], (BATCH, D), 1.0),
        "ctx": nrm(ks[2], (BATCH, CTX_LEN, D), 1.0),
        "c_ctx": nrm(ks[3], (D,), 1.0),
        "ada_w": nrm(ks[4], (DEPTH, D, N_MOD * D), 0.5 * D ** -0.5),
        "ada_b": nrm(ks[5], (DEPTH, N_MOD * D), 0.01),
        "norm_mix_g": gain(ks[6], (DEPTH, D)),
        "norm_ffn_g": gain(ks[7], (DEPTH, D)),
        "mla_wq_a": nrm(ks[8], (N_A, D, Q_LORA), D ** -0.5),
        "mla_q_a_norm": gain(ks[9], (N_A, Q_LORA)),
        "mla_wq_b": nrm(ks[10], (N_A, Q_LORA, MLA_HEADS * QK_DIM), Q_LORA ** -0.5),
        "mla_wkv_a": nrm(ks[11], (N_A, D, KV_LORA + QK_ROPE), D ** -0.5),
        "mla_kv_a_norm": gain(ks[12], (N_A, KV_LORA)),
        "mla_wkv_b": nrm(ks[13], (N_A, KV_LORA, MLA_HEADS * (QK_NOPE + V_DIM)), KV_LORA ** -0.5),
        "mla_q_norm": gain(ks[14], (N_A, QK_DIM)),
        "mla_k_norm": gain(ks[15], (N_A, QK_DIM)),
        "mla_wo": nrm(ks[16], (N_A, MLA_HEADS * V_DIM, D), (MLA_HEADS * V_DIM) ** -0.5),
        "gm_w_in": nrm(ks[17], (N_B, D, 2 * GM_HALF), D ** -0.5),
        "gm_ln_g": gain(ks[18], (N_B, GM_HALF)),
        "gm_ln_b": nrm(ks[19], (N_B, GM_HALF), 0.02),
        "gm_ws": nrm(ks[20], (N_B, GM_GROUPS, GM_CHUNK, GM_CHUNK), GM_CHUNK ** -0.5),
        "gm_bs": gain(ks[21], (N_B, GM_GROUPS, GM_CHUNK)),
        "gm_w_out": nrm(ks[22], (N_B, GM_HALF, D), GM_HALF ** -0.5),
        "ffn_w1": nrm(ks[23], (DEPTH, D, FFN_HIDDEN), D ** -0.5),
        "ffn_w2": nrm(ks[24], (DEPTH, FFN_HIDDEN, D), FFN_HIDDEN ** -0.5),
    }


def reference(x, c, ctx, c_ctx, ada_w, ada_b, norm_mix_g, norm_ffn_g,
              mla_wq_a, mla_q_a_norm, mla_wq_b, mla_wkv_a, mla_kv_a_norm, mla_wkv_b,
              mla_q_norm, mla_k_norm, mla_wo,
              gm_w_in, gm_ln_g, gm_ln_b, gm_ws, gm_bs, gm_w_out,
              ffn_w1, ffn_w2):
    S = x.shape[1]
    cos, sin = axial_rope_tables(S, x.dtype)
    silu_c = jax.nn.silu(c)
    silu_cc = jax.nn.silu(c_ctx)
    y = ctx
    for i in range(DEPTH):
        with_ctx = i < DEPTH - 1
        use_mla = (i % N_MIXERS) == 0
        j = i // N_MIXERS
        m_lat = (silu_c @ ada_w[i] + ada_b[i])[:, None, :]
        m_ctx = silu_cc @ ada_w[i] + ada_b[i]
        sh_m, sc_m, g_m, sh_f, sc_f, g_f = jnp.split(m_lat, N_MOD, axis=-1)
        csh_m, csc_m, cg_m, csh_f, csc_f, cg_f = jnp.split(m_ctx, N_MOD, axis=-1)

        h_lat = modulate(x, norm_mix_g[i], sh_m, sc_m)
        if use_mla:
            h_ctx = modulate(y, norm_mix_g[i], csh_m, csc_m)
            o_lat, o_ctx = mla_mixer(h_lat, h_ctx, cos, sin, mla_wq_a[j], mla_q_a_norm[j],
                                     mla_wq_b[j], mla_wkv_a[j], mla_kv_a_norm[j], mla_wkv_b[j],
                                     mla_q_norm[j], mla_k_norm[j], mla_wo[j], with_ctx)
        else:
            o_lat = gmlp_mixer(h_lat, gm_w_in[j], gm_ln_g[j], gm_ln_b[j], gm_ws[j], gm_bs[j],
                               gm_w_out[j])
            o_ctx = None
            if with_ctx:
                h_ctx = modulate(y, norm_mix_g[i], csh_m, csc_m)
                o_ctx = gmlp_mixer(h_ctx, gm_w_in[j], gm_ln_g[j], gm_ln_b[j], gm_ws[j], gm_bs[j],
                                   gm_w_out[j])
        x = x + g_m * o_lat
        x = x + g_f * squared_relu_mlp(modulate(x, norm_ffn_g[i], sh_f, sc_f), ffn_w1[i], ffn_w2[i])
        if with_ctx:
            y = y + cg_m * o_ctx
            y = y + cg_f * squared_relu_mlp(modulate(y, norm_ffn_g[i], csh_f, csc_f),
                                            ffn_w1[i], ffn_w2[i])
    return x
```

```python
import functools

import jax
import jax.numpy as jnp
import numpy as np
from jax import lax
from jax.experimental import pallas as pl
from jax.experimental.pallas import tpu as pltpu

F32 = jnp.float32
BF16 = jnp.bfloat16

D = 1024
B = 4
S = 4096
DEPTH = 4
GRID_W = 64
CTX = 256
H = 8
QK_NOPE = 128
QK_ROPE = 64
V_DIM = 128
QK_DIM = QK_NOPE + QK_ROPE
Q_LORA = 384
KV_LORA = 256
ROPE_THETA = 10000.0
GM_CHUNK = 128
GM_GROUPS = 8
GM_HALF = 2 * D
GM_GROUP_DIM = GM_HALF // GM_GROUPS
FFN_HIDDEN = 4 * D
N_MOD = 6
EPS = 1e-6

N_LAT = B * S
N_CTX = B * CTX
N_ALL = N_LAT + N_CTX
LANES = 128
HEAD_PAD = 2 * LANES
TM = 512
TQ = 512
TK = 512
MOD_ROWS = 8
CTX_MOD_ROW = B
LAT_TILES = N_LAT // TM
ALL_TILES = N_ALL // TM
TILES_PER_BATCH = S // TM
ADA_TN = 1536
FFN_CHUNK = 1024
VMEM_LIMIT = 56 * 1024 * 1024


def _mod_row(t):
    return jnp.where(t < LAT_TILES, t // TILES_PER_BATCH, CTX_MOD_ROW)


def _const_spec(shape):
    nd = len(shape)
    return pl.BlockSpec(shape, lambda *_: (0,) * nd, pipeline_mode=pl.Buffered(1))


def _rms(x):
    return x * lax.rsqrt(jnp.mean(x * x, axis=-1, keepdims=True) + EPS)


def _modulate(x, g, shift, scale):
    return (_rms(x) * g) * (1.0 + scale) + shift


def _dot(a, b):
    return jnp.dot(a, b, preferred_element_type=F32)


def _adaln_kernel(c_ref, w_ref, b_ref, o_ref):
    c = c_ref[...]
    s = c * jax.nn.sigmoid(c)
    o_ref[0] = _dot(s.astype(BF16), w_ref[0].astype(BF16)) + b_ref[0]


def _adaln(c8, ada_w, ada_b):
    return pl.pallas_call(
        _adaln_kernel,
        grid=(DEPTH, N_MOD * D // ADA_TN),
        in_specs=[
            pl.BlockSpec((MOD_ROWS, D), lambda i, j: (0, 0)),
            pl.BlockSpec((1, D, ADA_TN), lambda i, j: (i, 0, j)),
            pl.BlockSpec((1, 1, ADA_TN), lambda i, j: (i, 0, j)),
        ],
        out_specs=pl.BlockSpec((1, MOD_ROWS, ADA_TN), lambda i, j: (i, 0, j)),
        out_shape=jax.ShapeDtypeStruct((DEPTH, MOD_ROWS, N_MOD * D), F32),
        compiler_params=pltpu.CompilerParams(
            dimension_semantics=("arbitrary", "arbitrary"), vmem_limit_bytes=VMEM_LIMIT),
        name="adaln",
    )(c8, ada_w, ada_b.reshape(DEPTH, 1, N_MOD * D))


def _mla_proj_kernel(x_ref, mod_ref, g_ref, wqa_ref, qan_ref, wqb_ref, wkva_ref, kvan_ref,
                     wkvb_ref, gq_ref, gk_ref, cos_ref, sin_ref, q_ref, k_ref, v_ref):
    mod = mod_ref[0]
    h = _modulate(x_ref[...], g_ref[...], mod[:, 0:D], mod[:, D:2 * D]).astype(BF16)

    cq = _rms(_dot(h, wqa_ref[...])) * qan_ref[...]
    qf = _dot(cq.astype(BF16), wqb_ref[...])
    kva = _dot(h, wkva_ref[...])
    ckv = _rms(kva[:, 0:KV_LORA]) * kvan_ref[...]
    kvf = _dot(ckv.astype(BF16), wkvb_ref[...])

    cos = cos_ref[...]
    sin = sin_ref[...]
    gq = gq_ref[...]
    gk = gk_ref[...]
    inv_dim = 1.0 / QK_DIM

    def rope(pair, gains):
        return pair * (cos * gains[1:2]) + pltpu.roll(pair, QK_ROPE, axis=1) * (sin * gains[2:3])

    kpair = kva[:, KV_LORA:KV_LORA + LANES]
    k_rope = rope(kpair, gk)
    k_pair_sq = 0.5 * kpair * kpair
    for hd in range(H):
        kn = kvf[:, hd * LANES:(hd + 1) * LANES]
        ss = jnp.sum(kn * kn + k_pair_sq, axis=-1, keepdims=True)
        r = lax.rsqrt(ss * inv_dim + EPS)
        k_ref[hd, :, 0:LANES] = (kn * r * gk[0:1]).astype(BF16)
        k_ref[hd, :, LANES:HEAD_PAD] = (k_rope * r).astype(BF16)
        v_ref[hd] = kvf[:, (H + hd) * LANES:(H + hd + 1) * LANES].astype(BF16)

    sm_scale = QK_DIM ** -0.5
    for hd in range(H):
        qn = qf[:, hd * LANES:(hd + 1) * LANES]
        qpair = qf[:, (H + hd) * LANES:(H + hd + 1) * LANES]
        ss = jnp.sum(qn * qn + 0.5 * qpair * qpair, axis=-1, keepdims=True)
        r = lax.rsqrt(ss * inv_dim + EPS) * sm_scale
        q_ref[hd, :, 0:LANES] = (qn * r * gq[0:1]).astype(BF16)
        q_ref[hd, :, LANES:HEAD_PAD] = (rope(qpair, gq) * r).astype(BF16)


def _mla_proj(x_all, mods, layer, g, p, cos_t, sin_t):
    row = lambda t: (t, 0)
    return pl.pallas_call(
        _mla_proj_kernel,
        grid=(ALL_TILES,),
        in_specs=[
            pl.BlockSpec((TM, D), row),
            pl.BlockSpec((1, 1, N_MOD * D), lambda t: (layer * MOD_ROWS + _mod_row(t), 0, 0)),
            _const_spec((1, D)),
            _const_spec((D, Q_LORA)),
            _const_spec((1, Q_LORA)),
            _const_spec((Q_LORA, 2 * H * LANES)),
            _const_spec((D, KV_LORA + LANES)),
            _const_spec((1, KV_LORA)),
            _const_spec((KV_LORA, 2 * H * LANES)),
            _const_spec((8, LANES)),
            _const_spec((8, LANES)),
            pl.BlockSpec((TM, LANES), row),
            pl.BlockSpec((TM, LANES), row),
        ],
        out_specs=[
            pl.BlockSpec((H, TM, HEAD_PAD), lambda t: (0, t, 0)),
            pl.BlockSpec((H, TM, HEAD_PAD), lambda t: (0, t, 0)),
            pl.BlockSpec((H, TM, V_DIM), lambda t: (0, t, 0)),
        ],
        out_shape=[
            jax.ShapeDtypeStruct((H, N_ALL, HEAD_PAD), BF16),
            jax.ShapeDtypeStruct((H, N_ALL, HEAD_PAD), BF16),
            jax.ShapeDtypeStruct((H, N_ALL, V_DIM), BF16),
        ],
        compiler_params=pltpu.CompilerParams(
            dimension_semantics=("arbitrary",), vmem_limit_bytes=VMEM_LIMIT),
        name="mla_proj",
    )(x_all, mods, g, p["wqa"], p["qan"], p["wqb"], p["wkva"], p["kvan"], p["wkvb"],
      p["gq"], p["gk"], cos_t, sin_t)


def _softmax_step(q, k, v, m, l, acc):
    s = lax.dot_general(q, k, (((1,), (1,)), ((), ())), preferred_element_type=F32)
    m_new = jnp.maximum(m, jnp.max(s, axis=-1, keepdims=True))
    alpha = jnp.exp(m - m_new)
    p = jnp.exp(s - m_new)
    l = alpha * l + jnp.sum(p, axis=-1, keepdims=True)
    acc = alpha * acc + _dot(p.astype(BF16), v)
    return m_new, l, acc


def _attn_lat_kernel(q_ref, kl_ref, kc_ref, vl_ref, vc_ref, o_ref):
    q = q_ref[0]
    m = jnp.full((TQ, 1), -jnp.inf, F32)
    l = jnp.zeros((TQ, 1), F32)
    acc = jnp.zeros((TQ, V_DIM), F32)
    m, l, acc = _softmax_step(q, kc_ref[0], vc_ref[0], m, l, acc)

    def body(c, carry):
        off = pl.multiple_of(c * TK, TK)
        return _softmax_step(q, kl_ref[0, pl.ds(off, TK), :], vl_ref[0, pl.ds(off, TK), :], *carry)

    m, l, acc = lax.fori_loop(0, S // TK, body, (m, l, acc))
    o_ref[...] = (acc / l).astype(BF16)


def _attn_lat(q, k, v):
    ctx_blk = N_LAT // CTX
    return pl.pallas_call(
        _attn_lat_kernel,
        grid=(B, H, S // TQ),
        in_specs=[
            pl.BlockSpec((1, TQ, HEAD_PAD), lambda b, h, i: (h, b * (S // TQ) + i, 0)),
            pl.BlockSpec((1, S, HEAD_PAD), lambda b, h, i: (h, b, 0)),
            pl.BlockSpec((1, CTX, HEAD_PAD), lambda b, h, i: (h, ctx_blk + b, 0)),
            pl.BlockSpec((1, S, V_DIM), lambda b, h, i: (h, b, 0)),
            pl.BlockSpec((1, CTX, V_DIM), lambda b, h, i: (h, ctx_blk + b, 0)),
        ],
        out_specs=pl.BlockSpec((TQ, V_DIM), lambda b, h, i: (b * (S // TQ) + i, h)),
        out_shape=jax.ShapeDtypeStruct((N_ALL, H * V_DIM), BF16),
        compiler_params=pltpu.CompilerParams(
            dimension_semantics=("arbitrary", "arbitrary", "arbitrary"),
            vmem_limit_bytes=VMEM_LIMIT),
        name="attn_lat",
    )(q, k, k, v, v)


def _attn_ctx_kernel(q_ref, k_ref, v_ref, o_in_ref, o_ref):
    del o_in_ref
    s = lax.dot_general(q_ref[0], k_ref[0], (((1,), (1,)), ((), ())), preferred_element_type=F32)
    p = jnp.exp(s - jnp.max(s, axis=-1, keepdims=True))
    l = jnp.sum(p, axis=-1, keepdims=True)
    o_ref[...] = (_dot(p.astype(BF16), v_ref[0]) / l).astype(BF16)


def _attn_ctx(q, k, v, o):
    ctx_blk = N_LAT // CTX
    blk = lambda b, h: (h, ctx_blk + b, 0)
    return pl.pallas_call(
        _attn_ctx_kernel,
        grid=(B, H),
        in_specs=[
            pl.BlockSpec((1, CTX, HEAD_PAD), blk),
            pl.BlockSpec((1, CTX, HEAD_PAD), blk),
            pl.BlockSpec((1, CTX, V_DIM), blk),
            pl.BlockSpec(memory_space=pl.ANY),
        ],
        out_specs=pl.BlockSpec((CTX, V_DIM), lambda b, h: (ctx_blk + b, h)),
        out_shape=jax.ShapeDtypeStruct((N_ALL, H * V_DIM), BF16),
        input_output_aliases={3: 0},
        compiler_params=pltpu.CompilerParams(
            dimension_semantics=("arbitrary", "arbitrary"), vmem_limit_bytes=VMEM_LIMIT),
        name="attn_ctx",
    )(q, k, v, o)


def _gmlp_kernel(x_ref, mod_ref, g_ref, win_ref, lng_ref, lnb_ref, ws_ref, bs_ref, a_ref):
    mod = mod_ref[0]
    h = _modulate(x_ref[...], g_ref[...], mod[:, 0:D], mod[:, D:2 * D]).astype(BF16)
    sqrt_half = np.float32(np.sqrt(0.5))
    for half in range(2):
        z = _dot(h, win_ref[:, half * GM_HALF:(half + 1) * GM_HALF])
        z = 0.5 * z * (1.0 + lax.erf(z * sqrt_half))
        if half == 0:
            u = z
        else:
            mu = jnp.mean(z, axis=-1, keepdims=True)
            zc = z - mu
            var = jnp.mean(zc * zc, axis=-1, keepdims=True)
            vn = (zc * lax.rsqrt(var + EPS) * lng_ref[...] + lnb_ref[...]).astype(BF16)
    for c in range(TM // GM_CHUNK):
        rows = slice(c * GM_CHUNK, (c + 1) * GM_CHUNK)
        for gi in range(GM_GROUPS):
            cols = slice(gi * GM_GROUP_DIM, (gi + 1) * GM_GROUP_DIM)
            mixed = _dot(ws_ref[gi], vn[rows, cols]) + bs_ref[gi]
            a_ref[rows, cols] = (u[rows, cols] * mixed).astype(BF16)


def _gmlp(x_all, mods, layer, g, p, n_tiles):
    row = lambda t: (t, 0)
    return pl.pallas_call(
        _gmlp_kernel,
        grid=(n_tiles,),
        in_specs=[
            pl.BlockSpec((TM, D), row),
            pl.BlockSpec((1, 1, N_MOD * D), lambda t: (layer * MOD_ROWS + _mod_row(t), 0, 0)),
            _const_spec((1, D)),
            _const_spec((D, 2 * GM_HALF)),
            _const_spec((1, GM_HALF)),
            _const_spec((1, GM_HALF)),
            _const_spec((GM_GROUPS, GM_CHUNK, GM_CHUNK)),
            _const_spec((GM_GROUPS, GM_CHUNK, 1)),
        ],
        out_specs=pl.BlockSpec((TM, GM_HALF), row),
        out_shape=jax.ShapeDtypeStruct((n_tiles * TM, GM_HALF), BF16),
        compiler_params=pltpu.CompilerParams(
            dimension_semantics=("arbitrary",), vmem_limit_bytes=VMEM_LIMIT),
        name="gmlp_mix",
    )(x_all, mods, g, p["w_in"], p["ln_g"], p["ln_b"], p["ws"], p["bs"])


def _out_ffn_kernel(x_ref, a_ref, mod_ref, wo_ref, g_ref, w1_ref, w2_ref, o_ref):
    mod = mod_ref[0]
    x1 = x_ref[...] + mod[:, 2 * D:3 * D] * _dot(a_ref[...], wo_ref[...])
    h = _modulate(x1, g_ref[...], mod[:, 3 * D:4 * D], mod[:, 4 * D:5 * D]).astype(BF16)
    acc = jnp.zeros((TM, D), F32)
    for c in range(FFN_HIDDEN // FFN_CHUNK):
        cols = slice(c * FFN_CHUNK, (c + 1) * FFN_CHUNK)
        hid = jnp.maximum(_dot(h, w1_ref[:, cols]), 0.0)
        acc = acc + _dot((hid * hid).astype(BF16), w2_ref[cols, :])
    o_ref[...] = x1 + mod[:, 5 * D:6 * D] * acc


def _out_ffn(x_all, a, mods, layer, wo, g, w1, w2, n_tiles):
    row = lambda t: (t, 0)
    ka = a.shape[1]
    return pl.pallas_call(
        _out_ffn_kernel,
        grid=(n_tiles,),
        in_specs=[
            pl.BlockSpec((TM, D), row),
            pl.BlockSpec((TM, ka), row),
            pl.BlockSpec((1, 1, N_MOD * D), lambda t: (layer * MOD_ROWS + _mod_row(t), 0, 0)),
            _const_spec((ka, D)),
            _const_spec((1, D)),
            _const_spec((D, FFN_HIDDEN)),
            _const_spec((FFN_HIDDEN, D)),
        ],
        out_specs=pl.BlockSpec((TM, D), row),
        out_shape=jax.ShapeDtypeStruct((n_tiles * TM, D), F32),
        compiler_params=pltpu.CompilerParams(
            dimension_semantics=("arbitrary",), vmem_limit_bytes=VMEM_LIMIT),
        name="out_ffn",
    )(x_all, a, mods, wo, g, w1, w2)


def _rot_perm_sign():
    q = QK_ROPE // 4
    perm = np.concatenate([np.arange(q, 2 * q), np.arange(0, q), np.arange(3 * q, 4 * q), np.arange(2 * q, 3 * q)])
    sign = np.concatenate([-np.ones(q), np.ones(q), -np.ones(q), np.ones(q)]).astype(np.float32)
    return perm, sign


def _gain_rows(gain):
    perm, _ = _rot_perm_sign()
    pad = jnp.zeros((LANES - QK_ROPE,), F32)
    rows = jnp.stack([gain[:QK_NOPE],
                      jnp.concatenate([gain[QK_NOPE:], pad]),
                      jnp.concatenate([gain[QK_NOPE:][perm], pad])])
    return jnp.concatenate([rows, jnp.zeros((8 - 3, LANES), F32)])


def _mla_params(wq_a, q_a_norm, wq_b, wkv_a, kv_a_norm, wkv_b, q_norm, k_norm):
    perm, sign = _rot_perm_sign()
    wq_b = wq_b.reshape(Q_LORA, H, QK_DIM)
    q_rope = wq_b[:, :, QK_NOPE:]
    q_pair = jnp.concatenate([q_rope, q_rope[:, :, perm] * sign], axis=-1)
    wqb = jnp.concatenate([wq_b[:, :, :QK_NOPE].reshape(Q_LORA, H * QK_NOPE),
                           q_pair.reshape(Q_LORA, H * LANES)], axis=1)
    k_rope = wkv_a[:, KV_LORA:]
    wkva = jnp.concatenate([wkv_a, k_rope[:, perm] * sign], axis=1)
    wkv_b = wkv_b.reshape(KV_LORA, H, QK_NOPE + V_DIM)
    wkvb = jnp.concatenate([wkv_b[:, :, :QK_NOPE].reshape(KV_LORA, H * QK_NOPE),
                            wkv_b[:, :, QK_NOPE:].reshape(KV_LORA, H * V_DIM)], axis=1)
    return dict(wqa=wq_a.astype(BF16), qan=q_a_norm.reshape(1, Q_LORA), wqb=wqb.astype(BF16),
                wkva=wkva.astype(BF16), kvan=kv_a_norm.reshape(1, KV_LORA), wkvb=wkvb.astype(BF16),
                gq=_gain_rows(q_norm), gk=_gain_rows(k_norm))


def _rope_tables():
    row = jnp.repeat(jnp.arange(S // GRID_W, dtype=F32), GRID_W)
    col = jnp.tile(jnp.arange(GRID_W, dtype=F32), S // GRID_W)
    half = QK_ROPE // 2
    inv = ROPE_THETA ** (-jnp.arange(0, half, 2, dtype=F32) / half)
    ang_r = row[:, None] * inv[None, :]
    ang_c = col[:, None] * inv[None, :]
    ang = jnp.concatenate([ang_r, ang_r, ang_c, ang_c], axis=-1)
    pad = jnp.zeros((N_ALL, LANES - QK_ROPE), F32)
    cos = jnp.concatenate([jnp.tile(jnp.cos(ang), (B, 1)), jnp.ones((N_CTX, QK_ROPE), F32)])
    sin = jnp.concatenate([jnp.tile(jnp.sin(ang), (B, 1)), jnp.zeros((N_CTX, QK_ROPE), F32)])
    return jnp.concatenate([cos, pad], axis=1), jnp.concatenate([sin, pad], axis=1)


def kernel(x, c, ctx, c_ctx, ada_w, ada_b, norm_mix_g, norm_ffn_g, mla_wq_a, mla_q_a_norm, mla_wq_b,
           mla_wkv_a, mla_kv_a_norm, mla_wkv_b, mla_q_norm, mla_k_norm, mla_wo, gm_w_in, gm_ln_g,
           gm_ln_b, gm_ws, gm_bs, gm_w_out, ffn_w1, ffn_w2):
    c8 = jnp.concatenate([c, c_ctx[None], jnp.zeros((MOD_ROWS - B - 1, D), F32)])
    mods = _adaln(c8, ada_w, ada_b).reshape(DEPTH * MOD_ROWS, 1, N_MOD * D)
    cos_t, sin_t = _rope_tables()
    xs = jnp.concatenate([x.reshape(N_LAT, D), ctx.reshape(N_CTX, D)])

    for i in range(DEPTH):
        last = i == DEPTH - 1
        j = i // 2
        g_mix = norm_mix_g[i].reshape(1, D)
        g_ffn = norm_ffn_g[i].reshape(1, D)
        w1 = ffn_w1[i].astype(BF16)
        w2 = ffn_w2[i].astype(BF16)
        if i % 2 == 0:
            p = _mla_params(mla_wq_a[j], mla_q_a_norm[j], mla_wq_b[j], mla_wkv_a[j], mla_kv_a_norm[j],
                            mla_wkv_b[j], mla_q_norm[j], mla_k_norm[j])
            q, k, v = _mla_proj(xs, mods, i, g_mix, p, cos_t, sin_t)
            a = _attn_lat(q, k, v)
            ctx_live = i + 2 < DEPTH
            if ctx_live:
                a = _attn_ctx(q, k, v, a)
            n_tiles = ALL_TILES if ctx_live else LAT_TILES
            xs = _out_ffn(xs, a, mods, i, mla_wo[j].astype(BF16), g_ffn, w1, w2, n_tiles)
        else:
            n_tiles = LAT_TILES if last else ALL_TILES
            p = dict(w_in=gm_w_in[j].astype(BF16), ln_g=gm_ln_g[j].reshape(1, GM_HALF),
                     ln_b=gm_ln_b[j].reshape(1, GM_HALF), ws=gm_ws[j].astype(BF16),
                     bs=gm_bs[j].reshape(GM_GROUPS, GM_CHUNK, 1))
            a = _gmlp(xs, mods, i, g_mix, p, n_tiles)
            xs = _out_ffn(xs, a, mods, i, gm_w_out[j].astype(BF16), g_ffn, w1, w2, n_tiles)
    return xs[:N_LAT].reshape(B, S, D)
```

```python
import functools

import jax
import jax.numpy as jnp
import numpy as np
from jax import lax
from jax.experimental import pallas as pl
from jax.experimental.pallas import tpu as pltpu

F32 = jnp.float32
BF16 = jnp.bfloat16

D = 1024
B = 4
S = 4096
DEPTH = 4
GRID_W = 64
CTX = 256
H = 8
QK_NOPE = 128
QK_ROPE = 64
V_DIM = 128
QK_DIM = QK_NOPE + QK_ROPE
Q_LORA = 384
KV_LORA = 256
ROPE_THETA = 10000.0
GM_CHUNK = 128
GM_GROUPS = 8
GM_HALF = 2 * D
GM_GROUP_DIM = GM_HALF // GM_GROUPS
FFN_HIDDEN = 4 * D
N_MOD = 6
EPS = 1e-6

N_LAT = B * S
N_CTX = B * CTX
N_ALL = N_LAT + N_CTX
LANES = 128
HEAD_PAD = 2 * LANES
TM = 512
PROJ_SUB = 512
TQ = 1024
TK = 1024
MOD_ROWS = 8
CTX_MOD_ROW = B
LAT_TILES = N_LAT // TM
ALL_TILES = N_ALL // TM
TILES_PER_BATCH = S // TM
ADA_TN = 1536
FFN_CHUNK = 1024
VMEM_LIMIT = 56 * 1024 * 1024


def _mod_spec(layer):
    def index(t):
        return (layer * MOD_ROWS + jnp.where(t < LAT_TILES, t // TILES_PER_BATCH, CTX_MOD_ROW), 0, 0)
    return pl.BlockSpec((1, 1, N_MOD * D), index)


def _const_spec(shape):
    nd = len(shape)
    return pl.BlockSpec(shape, lambda *_: (0,) * nd, pipeline_mode=pl.Buffered(1))


def _layer_spec(shape, layer):
    nd = len(shape)
    return pl.BlockSpec((1,) + tuple(shape), lambda *_: (layer,) + (0,) * nd,
                        pipeline_mode=pl.Buffered(1))


def _row_specs(xs):
    if isinstance(xs, tuple):
        return [pl.BlockSpec((TM, D), lambda t: (jnp.minimum(t, LAT_TILES - 1), 0)),
                pl.BlockSpec((TM, D), lambda t: (jnp.maximum(t - LAT_TILES, 0), 0))], list(xs)
    return [pl.BlockSpec((TM, D), lambda t: (t, 0))], [xs]


def _load_rows(x_refs):
    if len(x_refs) == 1:
        return x_refs[0][...]
    return jnp.where(pl.program_id(0) < LAT_TILES, x_refs[0][...], x_refs[1][...])


def _rms(x):
    return x * lax.rsqrt(jnp.mean(x * x, axis=-1, keepdims=True) + EPS)


def _modulate(x, g, shift, scale):
    return (_rms(x) * g) * (1.0 + scale) + shift


def _dot(a, b):
    return jnp.dot(a, b, preferred_element_type=F32)


def _adaln_kernel(c_ref, w_ref, b_ref, o_ref):
    c = c_ref[...]
    s = c * jax.nn.sigmoid(c)
    o_ref[0] = _dot(s.astype(BF16), w_ref[0].astype(BF16)) + b_ref[0]


def _adaln(c8, ada_w, ada_b):
    return pl.pallas_call(
        _adaln_kernel,
        grid=(DEPTH, N_MOD * D // ADA_TN),
        in_specs=[
            pl.BlockSpec((MOD_ROWS, D), lambda i, j: (0, 0)),
            pl.BlockSpec((1, D, ADA_TN), lambda i, j: (i, 0, j)),
            pl.BlockSpec((1, 1, ADA_TN), lambda i, j: (i, 0, j)),
        ],
        out_specs=pl.BlockSpec((1, MOD_ROWS, ADA_TN), lambda i, j: (i, 0, j)),
        out_shape=jax.ShapeDtypeStruct((DEPTH, MOD_ROWS, N_MOD * D), F32),
        compiler_params=pltpu.CompilerParams(
            dimension_semantics=("arbitrary", "arbitrary"), vmem_limit_bytes=VMEM_LIMIT),
        name="adaln",
    )(c8, ada_w, ada_b.reshape(DEPTH, 1, N_MOD * D))


def _mla_proj_kernel(*refs, n_x):
    x_refs = refs[:n_x]
    (mod_ref, g_ref, wqa_ref, qan_ref, wqb_ref, wkva_ref, kvan_ref, wkvb_ref, gq_ref, gk_ref,
     cos_ref, sin_ref, q_ref, k_ref, v_ref) = refs[n_x:]
    mod = mod_ref[0]
    x = _load_rows(x_refs)
    gq = gq_ref[0]
    gk = gk_ref[0]
    inv_dim = 1.0 / QK_DIM
    sm_scale = QK_DIM ** -0.5 * np.log2(np.e)

    for sub in range(TM // PROJ_SUB):
        rows = slice(sub * PROJ_SUB, (sub + 1) * PROJ_SUB)
        h = _modulate(x[rows], g_ref[0], mod[:, 0:D], mod[:, D:2 * D]).astype(BF16)
        cq = _rms(_dot(h, wqa_ref[0])) * qan_ref[0]
        qf = _dot(cq.astype(BF16), wqb_ref[0])
        kva = _dot(h, wkva_ref[0])
        ckv = _rms(kva[:, 0:KV_LORA]) * kvan_ref[0]
        kvf = _dot(ckv.astype(BF16), wkvb_ref[0])
        cos = cos_ref[rows, :]
        sin = sin_ref[rows, :]

        def rope(pair, gains):
            return pair * (cos * gains[1:2]) + pltpu.roll(pair, QK_ROPE, axis=1) * (sin * gains[2:3])

        kpair = kva[:, KV_LORA:KV_LORA + LANES]
        k_rope = rope(kpair, gk)
        k_pair_sq = 0.5 * kpair * kpair
        for hd in range(H):
            kn = kvf[:, hd * LANES:(hd + 1) * LANES]
            ss = jnp.sum(kn * kn + k_pair_sq, axis=-1, keepdims=True)
            r = lax.rsqrt(ss * inv_dim + EPS)
            k_ref[hd, rows, 0:LANES] = (kn * r * gk[0:1]).astype(BF16)
            k_ref[hd, rows, LANES:HEAD_PAD] = (k_rope * r).astype(BF16)
            v_ref[hd, rows, :] = kvf[:, (H + hd) * LANES:(H + hd + 1) * LANES].astype(BF16)
        for hd in range(H):
            qn = qf[:, hd * LANES:(hd + 1) * LANES]
            qpair = qf[:, (H + hd) * LANES:(H + hd + 1) * LANES]
            ss = jnp.sum(qn * qn + 0.5 * qpair * qpair, axis=-1, keepdims=True)
            r = lax.rsqrt(ss * inv_dim + EPS) * sm_scale
            q_ref[hd, rows, 0:LANES] = (qn * r * gq[0:1]).astype(BF16)
            q_ref[hd, rows, LANES:HEAD_PAD] = (rope(qpair, gq) * r).astype(BF16)


def _mla_proj(xs, mods, layer, j, g, p, cos_t, sin_t):
    x_specs, x_args = _row_specs(xs)
    row = lambda t: (t, 0)
    return pl.pallas_call(
        functools.partial(_mla_proj_kernel, n_x=len(x_args)),
        grid=(ALL_TILES,),
        in_specs=x_specs + [
            _mod_spec(layer),
            _layer_spec((1, D), layer),
            _layer_spec((D, Q_LORA), j),
            _layer_spec((1, Q_LORA), j),
            _layer_spec((Q_LORA, 2 * H * LANES), j),
            _layer_spec((D, KV_LORA + LANES), j),
            _layer_spec((1, KV_LORA), j),
            _layer_spec((KV_LORA, 2 * H * LANES), j),
            _layer_spec((8, LANES), j),
            _layer_spec((8, LANES), j),
            pl.BlockSpec((TM, LANES), row),
            pl.BlockSpec((TM, LANES), row),
        ],
        out_specs=[
            pl.BlockSpec((H, TM, HEAD_PAD), lambda t: (0, t, 0)),
            pl.BlockSpec((H, TM, HEAD_PAD), lambda t: (0, t, 0)),
            pl.BlockSpec((H, TM, V_DIM), lambda t: (0, t, 0)),
        ],
        out_shape=[
            jax.ShapeDtypeStruct((H, N_ALL, HEAD_PAD), BF16),
            jax.ShapeDtypeStruct((H, N_ALL, HEAD_PAD), BF16),
            jax.ShapeDtypeStruct((H, N_ALL, V_DIM), BF16),
        ],
        compiler_params=pltpu.CompilerParams(
            dimension_semantics=("arbitrary",), vmem_limit_bytes=VMEM_LIMIT),
        name="mla_proj",
    )(*x_args, mods, g, p["wqa"], p["qan"], p["wqb"], p["wkva"], p["kvan"], p["wkvb"],
      p["gq"], p["gk"], cos_t, sin_t)


def _ones_column(rows):
    return (lax.broadcasted_iota(jnp.int32, (rows, LANES), 1) == 0).astype(BF16)


def _scores(q, k):
    return lax.dot_general(q, k, (((1,), (1,)), ((), ())), preferred_element_type=F32)


def _softmax_step(s, v, m, acc):
    m_new = jnp.max(s, axis=-1, keepdims=True)
    if m is not None:
        m_new = jnp.maximum(m, m_new)
    p = jnp.exp2(s - m_new).astype(BF16)
    pv = _dot(p, jnp.concatenate([v, _ones_column(v.shape[0])], axis=1))
    if m is None:
        return m_new, pv
    return m_new, jnp.exp2(m - m_new) * acc + pv


def _attn_lat_kernel(q_ref, kl_ref, kc_ref, vl_ref, vc_ref, o_ref):
    q = q_ref[0]
    m, acc = _softmax_step(_scores(q, kc_ref[0]), vc_ref[0], None, None)
    for c in range(S // TK):
        rows = slice(c * TK, (c + 1) * TK)
        m, acc = _softmax_step(_scores(q, kl_ref[0, rows, :]), vl_ref[0, rows, :], m, acc)
    o_ref[...] = (acc[:, 0:V_DIM] / acc[:, V_DIM:V_DIM + 1]).astype(BF16)


def _attn_lat(q, k, v):
    ctx_blk = N_LAT // CTX
    return pl.pallas_call(
        _attn_lat_kernel,
        grid=(B, H, S // TQ),
        in_specs=[
            pl.BlockSpec((1, TQ, HEAD_PAD), lambda b, h, i: (h, b * (S // TQ) + i, 0)),
            pl.BlockSpec((1, S, HEAD_PAD), lambda b, h, i: (h, b, 0)),
            pl.BlockSpec((1, CTX, HEAD_PAD), lambda b, h, i: (h, ctx_blk + b, 0)),
            pl.BlockSpec((1, S, V_DIM), lambda b, h, i: (h, b, 0)),
            pl.BlockSpec((1, CTX, V_DIM), lambda b, h, i: (h, ctx_blk + b, 0)),
        ],
        out_specs=pl.BlockSpec((TQ, V_DIM), lambda b, h, i: (b * (S // TQ) + i, h)),
        out_shape=jax.ShapeDtypeStruct((N_ALL, H * V_DIM), BF16),
        compiler_params=pltpu.CompilerParams(
            dimension_semantics=("arbitrary", "arbitrary", "arbitrary"),
            vmem_limit_bytes=VMEM_LIMIT),
        name="attn_lat",
    )(q, k, k, v, v)


def _attn_ctx_kernel(q_ref, k_ref, v_ref, o_in_ref, o_ref):
    del o_in_ref
    _, acc = _softmax_step(_scores(q_ref[0], k_ref[0]), v_ref[0], None, None)
    o_ref[...] = (acc[:, 0:V_DIM] / acc[:, V_DIM:V_DIM + 1]).astype(BF16)


def _attn_ctx(q, k, v, o):
    ctx_blk = N_LAT // CTX
    blk = lambda b, h: (h, ctx_blk + b, 0)
    return pl.pallas_call(
        _attn_ctx_kernel,
        grid=(B, H),
        in_specs=[
            pl.BlockSpec((1, CTX, HEAD_PAD), blk),
            pl.BlockSpec((1, CTX, HEAD_PAD), blk),
            pl.BlockSpec((1, CTX, V_DIM), blk),
            pl.BlockSpec(memory_space=pl.ANY),
        ],
        out_specs=pl.BlockSpec((CTX, V_DIM), lambda b, h: (ctx_blk + b, h)),
        out_shape=jax.ShapeDtypeStruct((N_ALL, H * V_DIM), BF16),
        input_output_aliases={3: 0},
        compiler_params=pltpu.CompilerParams(
            dimension_semantics=("arbitrary", "arbitrary"), vmem_limit_bytes=VMEM_LIMIT),
        name="attn_ctx",
    )(q, k, v, o)


def _gmlp_kernel(x_ref, mod_ref, g_ref, win_ref, lng_ref, lnb_ref, ws_ref, bs_ref, a_ref):
    mod = mod_ref[0]
    h = _modulate(x_ref[...], g_ref[0], mod[:, 0:D], mod[:, D:2 * D]).astype(BF16)
    sqrt_half = np.float32(np.sqrt(0.5))
    for half in range(2):
        z = _dot(h, win_ref[0, :, half * GM_HALF:(half + 1) * GM_HALF])
        z = 0.5 * z * (1.0 + lax.erf(z * sqrt_half))
        if half == 0:
            u = z
        else:
            mu = jnp.mean(z, axis=-1, keepdims=True)
            zc = z - mu
            var = jnp.mean(zc * zc, axis=-1, keepdims=True)
            vn = (zc * lax.rsqrt(var + EPS) * lng_ref[0] + lnb_ref[0]).astype(BF16)
    for c in range(TM // GM_CHUNK):
        rows = slice(c * GM_CHUNK, (c + 1) * GM_CHUNK)
        for gi in range(GM_GROUPS):
            cols = slice(gi * GM_GROUP_DIM, (gi + 1) * GM_GROUP_DIM)
            mixed = _dot(ws_ref[0, gi], vn[rows, cols]) + bs_ref[0, gi]
            a_ref[rows, cols] = (u[rows, cols] * mixed).astype(BF16)


def _gmlp(xs, mods, layer, j, g, p, n_tiles):
    row = lambda t: (t, 0)
    return pl.pallas_call(
        _gmlp_kernel,
        grid=(n_tiles,),
        in_specs=[
            pl.BlockSpec((TM, D), row),
            _mod_spec(layer),
            _layer_spec((1, D), layer),
            _layer_spec((D, 2 * GM_HALF), j),
            _layer_spec((1, GM_HALF), j),
            _layer_spec((1, GM_HALF), j),
            _layer_spec((GM_GROUPS, GM_CHUNK, GM_CHUNK), j),
            _layer_spec((GM_GROUPS, GM_CHUNK, 1), j),
        ],
        out_specs=pl.BlockSpec((TM, GM_HALF), row),
        out_shape=jax.ShapeDtypeStruct((n_tiles * TM, GM_HALF), BF16),
        compiler_params=pltpu.CompilerParams(
            dimension_semantics=("arbitrary",), vmem_limit_bytes=VMEM_LIMIT),
        name="gmlp_mix",
    )(xs, mods, g, p["w_in"], p["ln_g"], p["ln_b"], p["ws"], p["bs"])


def _out_ffn_kernel(*refs, n_x):
    x_refs = refs[:n_x]
    a_ref, mod_ref, wo_ref, g_ref, w1_ref, w2_ref, o_ref = refs[n_x:]
    mod = mod_ref[0]
    x1 = _load_rows(x_refs) + mod[:, 2 * D:3 * D] * _dot(a_ref[...], wo_ref[0])
    h = _modulate(x1, g_ref[0], mod[:, 3 * D:4 * D], mod[:, 4 * D:5 * D]).astype(BF16)
    acc = jnp.zeros((TM, D), F32)
    for c in range(FFN_HIDDEN // FFN_CHUNK):
        cols = slice(c * FFN_CHUNK, (c + 1) * FFN_CHUNK)
        hid = jnp.maximum(_dot(h, w1_ref[0, :, cols]), 0.0)
        acc = acc + _dot((hid * hid).astype(BF16), w2_ref[0, cols, :])
    o_ref[...] = x1 + mod[:, 5 * D:6 * D] * acc


def _out_ffn(xs, a, mods, layer, j, wo, g, w1, w2, n_tiles):
    x_specs, x_args = _row_specs(xs)
    row = lambda t: (t, 0)
    ka = a.shape[1]
    return pl.pallas_call(
        functools.partial(_out_ffn_kernel, n_x=len(x_args)),
        grid=(n_tiles,),
        in_specs=x_specs + [
            pl.BlockSpec((TM, ka), row),
            _mod_spec(layer),
            _layer_spec((ka, D), j),
            _layer_spec((1, D), layer),
            _layer_spec((D, FFN_HIDDEN), layer),
            _layer_spec((FFN_HIDDEN, D), layer),
        ],
        out_specs=pl.BlockSpec((TM, D), row),
        out_shape=jax.ShapeDtypeStruct((n_tiles * TM, D), F32),
        compiler_params=pltpu.CompilerParams(
            dimension_semantics=("arbitrary",), vmem_limit_bytes=VMEM_LIMIT),
        name="out_ffn",
    )(*x_args, a, mods, wo, g, w1, w2)


def _rot_perm_sign():
    q = QK_ROPE // 4
    perm = np.concatenate([np.arange(q, 2 * q), np.arange(0, q), np.arange(3 * q, 4 * q), np.arange(2 * q, 3 * q)])
    sign = np.concatenate([-np.ones(q), np.ones(q), -np.ones(q), np.ones(q)]).astype(np.float32)
    return perm, sign


def _gain_rows(gain):
    perm, _ = _rot_perm_sign()
    n = gain.shape[0]
    pad = jnp.zeros((n, LANES - QK_ROPE), F32)
    rows = jnp.stack([gain[:, :QK_NOPE],
                      jnp.concatenate([gain[:, QK_NOPE:], pad], axis=1),
                      jnp.concatenate([gain[:, QK_NOPE:][:, perm], pad], axis=1)], axis=1)
    return jnp.concatenate([rows, jnp.zeros((n, 8 - 3, LANES), F32)], axis=1)


def _mla_params(wq_a, q_a_norm, wq_b, wkv_a, kv_a_norm, wkv_b, q_norm, k_norm):
    perm, sign = _rot_perm_sign()
    n = wq_a.shape[0]
    wq_b = wq_b.reshape(n, Q_LORA, H, QK_DIM)
    q_rope = wq_b[..., QK_NOPE:]
    q_pair = jnp.concatenate([q_rope, q_rope[..., perm] * sign], axis=-1)
    wqb = jnp.concatenate([wq_b[..., :QK_NOPE].reshape(n, Q_LORA, H * QK_NOPE),
                           q_pair.reshape(n, Q_LORA, H * LANES)], axis=-1)
    k_rope = wkv_a[..., KV_LORA:]
    wkva = jnp.concatenate([wkv_a, k_rope[..., perm] * sign], axis=-1)
    wkv_b = wkv_b.reshape(n, KV_LORA, H, QK_NOPE + V_DIM)
    wkvb = jnp.concatenate([wkv_b[..., :QK_NOPE].reshape(n, KV_LORA, H * QK_NOPE),
                            wkv_b[..., QK_NOPE:].reshape(n, KV_LORA, H * V_DIM)], axis=-1)
    return dict(wqa=wq_a.astype(BF16), qan=q_a_norm.reshape(n, 1, Q_LORA), wqb=wqb.astype(BF16),
                wkva=wkva.astype(BF16), kvan=kv_a_norm.reshape(n, 1, KV_LORA), wkvb=wkvb.astype(BF16),
                gq=_gain_rows(q_norm), gk=_gain_rows(k_norm))


def _rope_tables():
    row = jnp.repeat(jnp.arange(S // GRID_W, dtype=F32), GRID_W)
    col = jnp.tile(jnp.arange(GRID_W, dtype=F32), S // GRID_W)
    half = QK_ROPE // 2
    inv = ROPE_THETA ** (-jnp.arange(0, half, 2, dtype=F32) / half)
    ang_r = row[:, None] * inv[None, :]
    ang_c = col[:, None] * inv[None, :]
    ang = jnp.concatenate([ang_r, ang_r, ang_c, ang_c], axis=-1)
    pad = jnp.zeros((N_ALL, LANES - QK_ROPE), F32)
    cos = jnp.concatenate([jnp.tile(jnp.cos(ang), (B, 1)), jnp.ones((N_CTX, QK_ROPE), F32)])
    sin = jnp.concatenate([jnp.tile(jnp.sin(ang), (B, 1)), jnp.zeros((N_CTX, QK_ROPE), F32)])
    return jnp.concatenate([cos, pad], axis=1), jnp.concatenate([sin, pad], axis=1)


def kernel(x, c, ctx, c_ctx, ada_w, ada_b, norm_mix_g, norm_ffn_g, mla_wq_a, mla_q_a_norm, mla_wq_b,
           mla_wkv_a, mla_kv_a_norm, mla_wkv_b, mla_q_norm, mla_k_norm, mla_wo, gm_w_in, gm_ln_g,
           gm_ln_b, gm_ws, gm_bs, gm_w_out, ffn_w1, ffn_w2):
    c8 = jnp.concatenate([c, c_ctx[None], jnp.zeros((MOD_ROWS - B - 1, D), F32)])
    mods = _adaln(c8, ada_w, ada_b).reshape(DEPTH * MOD_ROWS, 1, N_MOD * D)
    cos_t, sin_t = _rope_tables()

    g_mix = norm_mix_g.reshape(DEPTH, 1, D)
    g_ffn = norm_ffn_g.reshape(DEPTH, 1, D)
    w1 = ffn_w1.astype(BF16)
    w2 = ffn_w2.astype(BF16)
    mla = _mla_params(mla_wq_a, mla_q_a_norm, mla_wq_b, mla_wkv_a, mla_kv_a_norm, mla_wkv_b,
                      mla_q_norm, mla_k_norm)
    mla_wo = mla_wo.astype(BF16)
    gm = dict(w_in=gm_w_in.astype(BF16), ln_g=gm_ln_g.reshape(-1, 1, GM_HALF),
              ln_b=gm_ln_b.reshape(-1, 1, GM_HALF), ws=gm_ws.astype(BF16),
              bs=gm_bs.reshape(-1, GM_GROUPS, GM_CHUNK, 1))
    gm_w_out = gm_w_out.astype(BF16)

    xs = (x.reshape(N_LAT, D), ctx.reshape(N_CTX, D))
    for i in range(DEPTH):
        j = i // 2
        if i % 2 == 0:
            q, k, v = _mla_proj(xs, mods, i, j, g_mix, mla, cos_t, sin_t)
            a = _attn_lat(q, k, v)
            ctx_live = i + 2 < DEPTH
            if ctx_live:
                a = _attn_ctx(q, k, v, a)
            n_tiles = ALL_TILES if ctx_live else LAT_TILES
            xs = _out_ffn(xs, a, mods, i, j, mla_wo, g_ffn, w1, w2, n_tiles)
        else:
            n_tiles = LAT_TILES if i == DEPTH - 1 else ALL_TILES
            a = _gmlp(xs, mods, i, j, g_mix, gm, n_tiles)
            xs = _out_ffn(xs, a, mods, i, j, gm_w_out, g_ffn, w1, w2, n_tiles)
    return xs[:N_LAT].reshape(B, S, D)
```

```python
import functools

import jax
import jax.numpy as jnp
import numpy as np
from jax import lax
from jax.experimental import pallas as pl
from jax.experimental.pallas import tpu as pltpu

F32 = jnp.float32
BF16 = jnp.bfloat16

D = 1024
B = 4
S = 4096
DEPTH = 4
GRID_W = 64
CTX = 256
H = 8
QK_NOPE = 128
QK_ROPE = 64
V_DIM = 128
QK_DIM = QK_NOPE + QK_ROPE
Q_LORA = 384
KV_LORA = 256
ROPE_THETA = 10000.0
GM_CHUNK = 128
GM_GROUPS = 8
GM_HALF = 2 * D
GM_GROUP_DIM = GM_HALF // GM_GROUPS
FFN_HIDDEN = 4 * D
N_MOD = 6
EPS = 1e-6

N_LAT = B * S
N_CTX = B * CTX
N_ALL = N_LAT + N_CTX
LANES = 128
HEAD_PAD = 2 * LANES
TM = 512
TQ = 1024
TK = 512
MOD_ROWS = 8
CTX_MOD_ROW = B
LAT_TILES = N_LAT // TM
ALL_TILES = N_ALL // TM
TILES_PER_BATCH = S // TM
ADA_TN = 1536
FFN_CHUNK = 1024
GM_VCHUNK = 512
VMEM_LIMIT = 56 * 1024 * 1024


def _params(grid_rank):
    return pltpu.CompilerParams(dimension_semantics=("arbitrary",) * grid_rank,
                                vmem_limit_bytes=VMEM_LIMIT)


def _mod_spec(layer):
    def index(t):
        return (layer * MOD_ROWS + jnp.where(t < LAT_TILES, t // TILES_PER_BATCH, CTX_MOD_ROW), 0, 0)
    return pl.BlockSpec((1, 1, N_MOD * D), index)


def _const_spec(shape):
    nd = len(shape)
    return pl.BlockSpec(shape, lambda *_: (0,) * nd, pipeline_mode=pl.Buffered(1))


def _layer_spec(shape, layer):
    nd = len(shape)
    return pl.BlockSpec((1,) + tuple(shape), lambda *_: (layer,) + (0,) * nd,
                        pipeline_mode=pl.Buffered(1))


def _row_specs(xs):
    if isinstance(xs, tuple):
        width = xs[0].shape[1]
        return [pl.BlockSpec((TM, width), lambda t: (jnp.minimum(t, LAT_TILES - 1), 0)),
                pl.BlockSpec((TM, width), lambda t: (jnp.maximum(t - LAT_TILES, 0), 0))], list(xs)
    return [pl.BlockSpec((TM, xs.shape[1]), lambda t: (t, 0))], [xs]


def _load_rows(x_refs):
    if len(x_refs) == 1:
        return x_refs[0][...]
    return jnp.where(pl.program_id(0) < LAT_TILES, x_refs[0][...], x_refs[1][...])


def _rms(x):
    return x * lax.rsqrt(jnp.mean(x * x, axis=-1, keepdims=True) + EPS)


def _modulate(x, g, shift, scale):
    return (_rms(x) * g) * (1.0 + scale) + shift


def _dot(a, b):
    return jnp.dot(a, b, preferred_element_type=F32)


def _adaln_kernel(c_ref, w_ref, b_ref, o_ref):
    c = c_ref[...]
    s = c * jax.nn.sigmoid(c)
    o_ref[0] = _dot(s.astype(BF16), w_ref[0].astype(BF16)) + b_ref[0]


def _adaln(c8, ada_w, ada_b):
    return pl.pallas_call(
        _adaln_kernel,
        grid=(DEPTH, N_MOD * D // ADA_TN),
        in_specs=[
            pl.BlockSpec((MOD_ROWS, D), lambda i, j: (0, 0)),
            pl.BlockSpec((1, D, ADA_TN), lambda i, j: (i, 0, j)),
            pl.BlockSpec((1, 1, ADA_TN), lambda i, j: (i, 0, j)),
        ],
        out_specs=pl.BlockSpec((1, MOD_ROWS, ADA_TN), lambda i, j: (i, 0, j)),
        out_shape=jax.ShapeDtypeStruct((DEPTH, MOD_ROWS, N_MOD * D), F32),
        compiler_params=_params(2),
        name="adaln",
    )(c8, ada_w, ada_b.reshape(DEPTH, 1, N_MOD * D))


def _mla_proj_kernel(*refs, n_x):
    x_refs = refs[:n_x]
    (mod_ref, g_ref, wqa_ref, qan_ref, wqb_ref, wkva_ref, kvan_ref, wkvb_ref, gq_ref, gk_ref,
     cos_ref, sin_ref, q_ref, k_ref, v_ref) = refs[n_x:]
    mod = mod_ref[0]
    x = _load_rows(x_refs)
    gq = gq_ref[0]
    gk = gk_ref[0]
    inv_dim = 1.0 / QK_DIM
    sm_scale = QK_DIM ** -0.5 * np.log2(np.e)
    lane_head = lax.broadcasted_iota(jnp.int32, (HEAD_PAD, HEAD_PAD), 0) // LANES
    pair_ones = (lane_head == lax.broadcasted_iota(jnp.int32, (HEAD_PAD, HEAD_PAD), 1) // LANES).astype(BF16)

    h = _modulate(x, g_ref[0], mod[:, 0:D], mod[:, D:2 * D]).astype(BF16)
    cq = _rms(_dot(h, wqa_ref[0])) * qan_ref[0]
    qf = _dot(cq.astype(BF16), wqb_ref[0])
    kva = _dot(h, wkva_ref[0])
    ckv = _rms(kva[:, 0:KV_LORA]) * kvan_ref[0]
    kvf = _dot(ckv.astype(BF16), wkvb_ref[0])
    cos = cos_ref[...]
    sin = sin_ref[...]

    def rope(pair, gains):
        return pair * (cos * gains[1:2]) + pltpu.roll(pair, QK_ROPE, axis=1) * (sin * gains[2:3])

    kpair = kva[:, KV_LORA:KV_LORA + LANES]
    k_rope = rope(kpair, gk)
    k_pair_sq = 0.5 * kpair * kpair
    for hd in range(0, H, 2):
        kn = [kvf[:, (hd + i) * LANES:(hd + i + 1) * LANES] for i in range(2)]
        sq = jnp.concatenate([n * n + k_pair_sq for n in kn], axis=1)
        r = lax.rsqrt(_dot(sq.astype(BF16), pair_ones) * inv_dim + EPS)
        for i in range(2):
            ri = r[:, i * LANES:(i + 1) * LANES]
            k_ref[hd + i, :, 0:LANES] = (kn[i] * ri * gk[0:1]).astype(BF16)
            k_ref[hd + i, :, LANES:HEAD_PAD] = (k_rope * ri).astype(BF16)
            v_ref[hd + i] = kvf[:, (H + hd + i) * LANES:(H + hd + i + 1) * LANES].astype(BF16)
    for hd in range(0, H, 2):
        qn = [qf[:, (hd + i) * LANES:(hd + i + 1) * LANES] for i in range(2)]
        qpair = [qf[:, (H + hd + i) * LANES:(H + hd + i + 1) * LANES] for i in range(2)]
        sq = jnp.concatenate([n * n + 0.5 * pr * pr for n, pr in zip(qn, qpair)], axis=1)
        r = lax.rsqrt(_dot(sq.astype(BF16), pair_ones) * inv_dim + EPS) * sm_scale
        for i in range(2):
            ri = r[:, i * LANES:(i + 1) * LANES]
            q_ref[hd + i, :, 0:LANES] = (qn[i] * ri * gq[0:1]).astype(BF16)
            q_ref[hd + i, :, LANES:HEAD_PAD] = (rope(qpair[i], gq) * ri).astype(BF16)


def _mla_proj(xs, mods, layer, j, g, p, cos_t, sin_t):
    x_specs, x_args = _row_specs(xs)
    row = lambda t: (t, 0)
    return pl.pallas_call(
        functools.partial(_mla_proj_kernel, n_x=len(x_args)),
        grid=(ALL_TILES,),
        in_specs=x_specs + [
            _mod_spec(layer),
            _layer_spec((1, D), layer),
            _layer_spec((D, Q_LORA), j),
            _layer_spec((1, Q_LORA), j),
            _layer_spec((Q_LORA, 2 * H * LANES), j),
            _layer_spec((D, KV_LORA + LANES), j),
            _layer_spec((1, KV_LORA), j),
            _layer_spec((KV_LORA, 2 * H * LANES), j),
            _layer_spec((8, LANES), j),
            _layer_spec((8, LANES), j),
            pl.BlockSpec((TM, LANES), row),
            pl.BlockSpec((TM, LANES), row),
        ],
        out_specs=[
            pl.BlockSpec((H, TM, HEAD_PAD), lambda t: (0, t, 0)),
            pl.BlockSpec((H, TM, HEAD_PAD), lambda t: (0, t, 0)),
            pl.BlockSpec((H, TM, V_DIM), lambda t: (0, t, 0)),
        ],
        out_shape=[
            jax.ShapeDtypeStruct((H, N_ALL, HEAD_PAD), BF16),
            jax.ShapeDtypeStruct((H, N_ALL, HEAD_PAD), BF16),
            jax.ShapeDtypeStruct((H, N_ALL, V_DIM), BF16),
        ],
        compiler_params=_params(1),
        name="mla_proj",
    )(*x_args, mods, g, p["wqa"], p["qan"], p["wqb"], p["wkva"], p["kvan"], p["wkvb"],
      p["gq"], p["gk"], cos_t, sin_t)


def _ones_column(rows):
    return (lax.broadcasted_iota(jnp.int32, (rows, LANES), 1) == 0).astype(BF16)


def _scores(q, k):
    return lax.dot_general(q, k, (((1,), (1,)), ((), ())), preferred_element_type=F32)


def _softmax_step(s, v, m, acc):
    m_new = jnp.max(s, axis=-1, keepdims=True)
    if m is not None:
        m_new = jnp.maximum(m, m_new)
    p = jnp.exp2(s - m_new).astype(BF16)
    pv = _dot(p, jnp.concatenate([v, _ones_column(v.shape[0])], axis=1))
    if m is None:
        return m_new, pv
    return m_new, jnp.exp2(m - m_new) * acc + pv


def _attn_lat_kernel(q_ref, kl_ref, kc_ref, vl_ref, vc_ref, o_ref):
    q = q_ref[0]
    m = acc = None
    for c in range(S // TK):
        rows = slice(c * TK, (c + 1) * TK)
        m, acc = _softmax_step(_scores(q, kl_ref[0, rows, :]), vl_ref[0, rows, :], m, acc)
    m, acc = _softmax_step(_scores(q, kc_ref[0]), vc_ref[0], m, acc)
    o_ref[...] = (acc[:, 0:V_DIM] / acc[:, V_DIM:V_DIM + 1]).astype(BF16)


def _attn_lat(q, k, v):
    ctx_blk = N_LAT // CTX
    return pl.pallas_call(
        _attn_lat_kernel,
        grid=(B, H, S // TQ),
        in_specs=[
            pl.BlockSpec((1, TQ, HEAD_PAD), lambda b, h, i: (h, b * (S // TQ) + i, 0)),
            pl.BlockSpec((1, S, HEAD_PAD), lambda b, h, i: (h, b, 0)),
            pl.BlockSpec((1, CTX, HEAD_PAD), lambda b, h, i: (h, ctx_blk + b, 0)),
            pl.BlockSpec((1, S, V_DIM), lambda b, h, i: (h, b, 0)),
            pl.BlockSpec((1, CTX, V_DIM), lambda b, h, i: (h, ctx_blk + b, 0)),
        ],
        out_specs=pl.BlockSpec((TQ, V_DIM), lambda b, h, i: (b * (S // TQ) + i, h)),
        out_shape=jax.ShapeDtypeStruct((N_LAT, H * V_DIM), BF16),
        compiler_params=_params(3),
        name="attn_lat",
    )(q, k, k, v, v)


def _attn_ctx_kernel(q_ref, k_ref, v_ref, o_ref):
    for hd in range(H):
        _, acc = _softmax_step(_scores(q_ref[hd], k_ref[hd]), v_ref[hd], None, None)
        o_ref[:, hd * V_DIM:(hd + 1) * V_DIM] = (acc[:, 0:V_DIM] / acc[:, V_DIM:V_DIM + 1]).astype(BF16)


def _attn_ctx(q, k, v):
    blk = lambda b: (0, N_LAT // CTX + b, 0)
    return pl.pallas_call(
        _attn_ctx_kernel,
        grid=(B,),
        in_specs=[
            pl.BlockSpec((H, CTX, HEAD_PAD), blk),
            pl.BlockSpec((H, CTX, HEAD_PAD), blk),
            pl.BlockSpec((H, CTX, V_DIM), blk),
        ],
        out_specs=pl.BlockSpec((CTX, H * V_DIM), lambda b: (b, 0)),
        out_shape=jax.ShapeDtypeStruct((N_CTX, H * V_DIM), BF16),
        compiler_params=_params(1),
        name="attn_ctx",
    )(q, k, v)


def _gmlp_kernel(x_ref, mod_ref, g_ref, win_ref, lng_ref, lnb_ref, ws_ref, bs_ref, a_ref):
    mod = mod_ref[0]
    h = _modulate(x_ref[...], g_ref[0], mod[:, 0:D], mod[:, D:2 * D]).astype(BF16)
    sqrt_half = np.float32(np.sqrt(0.5))

    def gelu2(z):
        return z * (1.0 + lax.erf(z * sqrt_half))

    def lane_partial(a):
        acc = a[:, 0:LANES]
        for i in range(1, a.shape[1] // LANES):
            acc = acc + a[:, i * LANES:(i + 1) * LANES]
        return acc

    zks = []
    for c in range(GM_HALF // GM_VCHUNK):
        z = gelu2(_dot(h, win_ref[0, :, GM_HALF + c * GM_VCHUNK:GM_HALF + (c + 1) * GM_VCHUNK]))
        if c == 0:
            k = jnp.mean(z, axis=-1, keepdims=True)
        zk = z - k
        zks.append(zk)
        s1 = lane_partial(zk) if c == 0 else s1 + lane_partial(zk)
        s2 = lane_partial(zk * zk) if c == 0 else s2 + lane_partial(zk * zk)
    d = jnp.sum(s1, axis=-1, keepdims=True) * (1.0 / GM_HALF)
    var = jnp.sum(s2, axis=-1, keepdims=True) * (1.0 / GM_HALF) - d * d
    r = lax.rsqrt(var + 4.0 * EPS)
    vn = [((zk - d) * r).astype(BF16) for zk in zks]

    per_chunk = GM_VCHUNK // GM_GROUP_DIM
    for gi in range(GM_GROUPS):
        cols = slice(gi * GM_GROUP_DIM, (gi + 1) * GM_GROUP_DIM)
        u = gelu2(_dot(h, win_ref[0, :, cols]))
        ws = ws_ref[0, gi]
        bias = lnb_ref[0][:, cols] * jnp.sum(ws.astype(F32), axis=1, keepdims=True) + bs_ref[0, gi]
        gain = lng_ref[0][:, cols]
        vcols = slice((gi % per_chunk) * GM_GROUP_DIM, (gi % per_chunk + 1) * GM_GROUP_DIM)
        for c in range(TM // GM_CHUNK):
            rows = slice(c * GM_CHUNK, (c + 1) * GM_CHUNK)
            mixed = _dot(ws, vn[gi // per_chunk][rows, vcols]) * gain + bias
            a_ref[rows, cols] = (u[rows] * mixed).astype(BF16)


def _gmlp(xs, mods, layer, j, g, p, n_tiles):
    row = lambda t: (t, 0)
    return pl.pallas_call(
        _gmlp_kernel,
        grid=(n_tiles,),
        in_specs=[
            pl.BlockSpec((TM, D), row),
            _mod_spec(layer),
            _layer_spec((1, D), layer),
            _layer_spec((D, 2 * GM_HALF), j),
            _layer_spec((1, GM_HALF), j),
            _layer_spec((1, GM_HALF), j),
            _layer_spec((GM_GROUPS, GM_CHUNK, GM_CHUNK), j),
            _layer_spec((GM_GROUPS, GM_CHUNK, 1), j),
        ],
        out_specs=pl.BlockSpec((TM, GM_HALF), row),
        out_shape=jax.ShapeDtypeStruct((n_tiles * TM, GM_HALF), BF16),
        compiler_params=_params(1),
        name="gmlp_mix",
    )(xs, mods, g, p["w_in"], p["ln_g"], p["ln_b"], p["ws"], p["bs"])


def _out_ffn_kernel(*refs, n_x, n_a):
    x_refs, a_refs = refs[:n_x], refs[n_x:n_x + n_a]
    mod_ref, wo_ref, g_ref, w1_ref, w2_ref, o_ref = refs[n_x + n_a:]
    mod = mod_ref[0]
    x1 = _load_rows(x_refs) + mod[:, 2 * D:3 * D] * _dot(_load_rows(a_refs), wo_ref[0])
    h = _modulate(x1, g_ref[0], mod[:, 3 * D:4 * D], mod[:, 4 * D:5 * D]).astype(BF16)
    acc = jnp.zeros((TM, D), F32)
    for c in range(FFN_HIDDEN // FFN_CHUNK):
        cols = slice(c * FFN_CHUNK, (c + 1) * FFN_CHUNK)
        hid = jnp.maximum(_dot(h, w1_ref[0, :, cols]), 0.0)
        acc = acc + _dot((hid * hid).astype(BF16), w2_ref[0, cols, :])
    o_ref[...] = x1 + mod[:, 5 * D:6 * D] * acc


def _out_ffn(xs, a, mods, layer, j, wo, g, w1, w2, n_tiles):
    x_specs, x_args = _row_specs(xs)
    a_specs, a_args = _row_specs(a)
    row = lambda t: (t, 0)
    ka = a_args[0].shape[1]
    return pl.pallas_call(
        functools.partial(_out_ffn_kernel, n_x=len(x_args), n_a=len(a_args)),
        grid=(n_tiles,),
        in_specs=x_specs + a_specs + [
            _mod_spec(layer),
            _layer_spec((ka, D), j),
            _layer_spec((1, D), layer),
            _layer_spec((D, FFN_HIDDEN), layer),
            _layer_spec((FFN_HIDDEN, D), layer),
        ],
        out_specs=pl.BlockSpec((TM, D), row),
        out_shape=jax.ShapeDtypeStruct((n_tiles * TM, D), F32),
        compiler_params=_params(1),
        name="out_ffn",
    )(*x_args, *a_args, mods, wo, g, w1, w2)


def _rot_perm_sign():
    q = QK_ROPE // 4
    perm = np.concatenate([np.arange(q, 2 * q), np.arange(0, q), np.arange(3 * q, 4 * q), np.arange(2 * q, 3 * q)])
    sign = np.concatenate([-np.ones(q), np.ones(q), -np.ones(q), np.ones(q)]).astype(np.float32)
    return perm, sign


def _gain_rows(gain):
    perm, _ = _rot_perm_sign()
    n = gain.shape[0]
    pad = jnp.zeros((n, LANES - QK_ROPE), F32)
    rows = jnp.stack([gain[:, :QK_NOPE],
                      jnp.concatenate([gain[:, QK_NOPE:], pad], axis=1),
                      jnp.concatenate([gain[:, QK_NOPE:][:, perm], pad], axis=1)], axis=1)
    return jnp.concatenate([rows, jnp.zeros((n, 8 - 3, LANES), F32)], axis=1)


def _mla_params(wq_a, q_a_norm, wq_b, wkv_a, kv_a_norm, wkv_b, q_norm, k_norm):
    perm, sign = _rot_perm_sign()
    n = wq_a.shape[0]
    wq_b = wq_b.reshape(n, Q_LORA, H, QK_DIM)
    q_rope = wq_b[..., QK_NOPE:]
    q_pair = jnp.concatenate([q_rope, q_rope[..., perm] * sign], axis=-1)
    wqb = jnp.concatenate([wq_b[..., :QK_NOPE].reshape(n, Q_LORA, H * QK_NOPE),
                           q_pair.reshape(n, Q_LORA, H * LANES)], axis=-1)
    k_rope = wkv_a[..., KV_LORA:]
    wkva = jnp.concatenate([wkv_a, k_rope[..., perm] * sign], axis=-1)
    wkv_b = wkv_b.reshape(n, KV_LORA, H, QK_NOPE + V_DIM)
    wkvb = jnp.concatenate([wkv_b[..., :QK_NOPE].reshape(n, KV_LORA, H * QK_NOPE),
                            wkv_b[..., QK_NOPE:].reshape(n, KV_LORA, H * V_DIM)], axis=-1)
    return dict(wqa=wq_a.astype(BF16), qan=q_a_norm.reshape(n, 1, Q_LORA), wqb=wqb.astype(BF16),
                wkva=wkva.astype(BF16), kvan=kv_a_norm.reshape(n, 1, KV_LORA), wkvb=wkvb.astype(BF16),
                gq=_gain_rows(q_norm), gk=_gain_rows(k_norm))


def _rope_tables():
    row = jnp.repeat(jnp.arange(S // GRID_W, dtype=F32), GRID_W)
    col = jnp.tile(jnp.arange(GRID_W, dtype=F32), S // GRID_W)
    half = QK_ROPE // 2
    inv = ROPE_THETA ** (-jnp.arange(0, half, 2, dtype=F32) / half)
    ang_r = row[:, None] * inv[None, :]
    ang_c = col[:, None] * inv[None, :]
    ang = jnp.concatenate([ang_r, ang_r, ang_c, ang_c], axis=-1)
    pad = jnp.zeros((N_ALL, LANES - QK_ROPE), F32)
    cos = jnp.concatenate([jnp.tile(jnp.cos(ang), (B, 1)), jnp.ones((N_CTX, QK_ROPE), F32)])
    sin = jnp.concatenate([jnp.tile(jnp.sin(ang), (B, 1)), jnp.zeros((N_CTX, QK_ROPE), F32)])
    return jnp.concatenate([cos, pad], axis=1), jnp.concatenate([sin, pad], axis=1)


def kernel(x, c, ctx, c_ctx, ada_w, ada_b, norm_mix_g, norm_ffn_g, mla_wq_a, mla_q_a_norm, mla_wq_b,
           mla_wkv_a, mla_kv_a_norm, mla_wkv_b, mla_q_norm, mla_k_norm, mla_wo, gm_w_in, gm_ln_g,
           gm_ln_b, gm_ws, gm_bs, gm_w_out, ffn_w1, ffn_w2):
    c8 = jnp.concatenate([c, c_ctx[None], jnp.zeros((MOD_ROWS - B - 1, D), F32)])
    mods = _adaln(c8, ada_w, ada_b).reshape(DEPTH * MOD_ROWS, 1, N_MOD * D)
    cos_t, sin_t = _rope_tables()

    g_mix = norm_mix_g.reshape(DEPTH, 1, D)
    g_ffn = norm_ffn_g.reshape(DEPTH, 1, D)
    w1 = ffn_w1.astype(BF16)
    w2 = ffn_w2.astype(BF16)
    mla = _mla_params(mla_wq_a, mla_q_a_norm, mla_wq_b, mla_wkv_a, mla_kv_a_norm, mla_wkv_b,
                      mla_q_norm, mla_k_norm)
    mla_wo = mla_wo.astype(BF16)
    gm = dict(w_in=gm_w_in.astype(BF16), ln_g=gm_ln_g.reshape(-1, 1, GM_HALF),
              ln_b=gm_ln_b.reshape(-1, 1, GM_HALF), ws=(0.5 * gm_ws).astype(BF16),
              bs=(0.5 * gm_bs).reshape(-1, GM_GROUPS, GM_CHUNK, 1))
    gm_w_out = gm_w_out.astype(BF16)

    xs = (x.reshape(N_LAT, D), ctx.reshape(N_CTX, D))
    for i in range(DEPTH):
        j = i // 2
        if i % 2 == 0:
            q, k, v = _mla_proj(xs, mods, i, j, g_mix, mla, cos_t, sin_t)
            a = _attn_lat(q, k, v)
            ctx_live = i + 2 < DEPTH
            if ctx_live:
                a = (a, _attn_ctx(q, k, v))
            n_tiles = ALL_TILES if ctx_live else LAT_TILES
            xs = _out_ffn(xs, a, mods, i, j, mla_wo, g_ffn, w1, w2, n_tiles)
        else:
            n_tiles = LAT_TILES if i == DEPTH - 1 else ALL_TILES
            a = _gmlp(xs, mods, i, j, g_mix, gm, n_tiles)
            xs = _out_ffn(xs, a, mods, i, j, gm_w_out, g_ffn, w1, w2, n_tiles)
    return xs[:N_LAT].reshape(B, S, D)
```

```python
import functools

import jax
import jax.numpy as jnp
import numpy as np
from jax import lax
from jax.experimental import pallas as pl
from jax.experimental.pallas import tpu as pltpu

F32 = jnp.float32
BF16 = jnp.bfloat16

D = 1024
B = 4
S = 4096
DEPTH = 4
GRID_W = 64
CTX = 256
H = 8
QK_NOPE = 128
QK_ROPE = 64
V_DIM = 128
QK_DIM = QK_NOPE + QK_ROPE
Q_LORA = 384
KV_LORA = 256
ROPE_THETA = 10000.0
GM_CHUNK = 128
GM_GROUPS = 8
GM_HALF = 2 * D
GM_GROUP_DIM = GM_HALF // GM_GROUPS
FFN_HIDDEN = 4 * D
N_MOD = 6
EPS = 1e-6

N_LAT = B * S
N_CTX = B * CTX
N_ALL = N_LAT + N_CTX
LANES = 128
HEAD_PAD = 2 * LANES
TM = 512
TQ = 1024
TK = 512
ATTN_HEADS = 2
MOD_ROWS = 8
CTX_MOD_ROW = B
LAT_TILES = N_LAT // TM
ALL_TILES = N_ALL // TM
TILES_PER_BATCH = S // TM
ADA_TN = 1536
FFN_CHUNK = 1024
GM_VCHUNK = 512
VMEM_LIMIT = 56 * 1024 * 1024


def _params(grid_rank):
    return pltpu.CompilerParams(dimension_semantics=("arbitrary",) * grid_rank,
                                vmem_limit_bytes=VMEM_LIMIT)


def _mod_spec(layer):
    def index(t):
        return (layer * MOD_ROWS + jnp.where(t < LAT_TILES, t // TILES_PER_BATCH, CTX_MOD_ROW), 0, 0)
    return pl.BlockSpec((1, 1, N_MOD * D), index)


def _const_spec(shape):
    nd = len(shape)
    return pl.BlockSpec(shape, lambda *_: (0,) * nd, pipeline_mode=pl.Buffered(1))


def _layer_spec(shape, layer):
    nd = len(shape)
    return pl.BlockSpec((1,) + tuple(shape), lambda *_: (layer,) + (0,) * nd,
                        pipeline_mode=pl.Buffered(1))


def _row_specs(xs):
    if isinstance(xs, tuple):
        width = xs[0].shape[1]
        return [pl.BlockSpec((TM, width), lambda t: (jnp.minimum(t, LAT_TILES - 1), 0)),
                pl.BlockSpec((TM, width), lambda t: (jnp.maximum(t - LAT_TILES, 0), 0))], list(xs)
    return [pl.BlockSpec((TM, xs.shape[1]), lambda t: (t, 0))], [xs]


def _load_rows(x_refs):
    if len(x_refs) == 1:
        return x_refs[0][...]
    return jnp.where(pl.program_id(0) < LAT_TILES, x_refs[0][...], x_refs[1][...])


def _rms(x):
    return x * lax.rsqrt(jnp.mean(x * x, axis=-1, keepdims=True) + EPS)


def _modulate(x, g, shift, scale):
    return (_rms(x) * g) * (1.0 + scale) + shift


def _dot(a, b):
    return jnp.dot(a, b, preferred_element_type=F32)


def _adaln_kernel(c_ref, w_ref, b_ref, o_ref):
    c = c_ref[...]
    s = c * jax.nn.sigmoid(c)
    o_ref[0] = _dot(s.astype(BF16), w_ref[0].astype(BF16)) + b_ref[0]


def _adaln(c8, ada_w, ada_b):
    return pl.pallas_call(
        _adaln_kernel,
        grid=(DEPTH, N_MOD * D // ADA_TN),
        in_specs=[
            pl.BlockSpec((MOD_ROWS, D), lambda i, j: (0, 0)),
            pl.BlockSpec((1, D, ADA_TN), lambda i, j: (i, 0, j)),
            pl.BlockSpec((1, 1, ADA_TN), lambda i, j: (i, 0, j)),
        ],
        out_specs=pl.BlockSpec((1, MOD_ROWS, ADA_TN), lambda i, j: (i, 0, j)),
        out_shape=jax.ShapeDtypeStruct((DEPTH, MOD_ROWS, N_MOD * D), F32),
        compiler_params=_params(2),
        name="adaln",
    )(c8, ada_w, ada_b.reshape(DEPTH, 1, N_MOD * D))


def _mla_proj_kernel(*refs, n_x):
    x_refs = refs[:n_x]
    (mod_ref, g_ref, wqa_ref, qan_ref, wqb_ref, wkva_ref, kvan_ref, wkvb_ref, gq_ref, gk_ref,
     cos_ref, sin_ref, q_ref, k_ref, v_ref) = refs[n_x:]
    mod = mod_ref[0]
    x = _load_rows(x_refs)
    gq = gq_ref[0]
    gk = gk_ref[0]
    inv_dim = 1.0 / QK_DIM
    sm_scale = QK_DIM ** -0.5 * np.log2(np.e)
    lane_head = lax.broadcasted_iota(jnp.int32, (HEAD_PAD, HEAD_PAD), 0) // LANES
    pair_ones = (lane_head == lax.broadcasted_iota(jnp.int32, (HEAD_PAD, HEAD_PAD), 1) // LANES).astype(BF16)

    h = _modulate(x, g_ref[0], mod[:, 0:D], mod[:, D:2 * D]).astype(BF16)
    kva = _dot(h, wkva_ref[0])
    ckv = (_rms(kva[:, 0:KV_LORA]) * kvan_ref[0]).astype(BF16)
    cq = (_rms(_dot(h, wqa_ref[0])) * qan_ref[0]).astype(BF16)
    cos = cos_ref[...]
    sin = sin_ref[...]

    def rope(pair, gains):
        return pair * (cos * gains[1:2]) + pltpu.roll(pair, QK_ROPE, axis=1) * (sin * gains[2:3])

    kpair = kva[:, KV_LORA:KV_LORA + LANES]
    k_rope = rope(kpair, gk)
    k_pair_sq = 0.5 * kpair * kpair
    pair_cols = 4 * LANES
    for hd in range(0, H, 2):
        kv4 = _dot(ckv, wkvb_ref[0, :, hd // 2 * pair_cols:(hd // 2 + 1) * pair_cols])
        kn = [kv4[:, i * LANES:(i + 1) * LANES] for i in range(2)]
        sq = jnp.concatenate([n * n + k_pair_sq for n in kn], axis=1)
        r = lax.rsqrt(_dot(sq.astype(BF16), pair_ones) * inv_dim + EPS)
        for i in range(2):
            ri = r[:, i * LANES:(i + 1) * LANES]
            k_ref[hd + i, :, 0:LANES] = (kn[i] * ri * gk[0:1]).astype(BF16)
            k_ref[hd + i, :, LANES:HEAD_PAD] = (k_rope * ri).astype(BF16)
            v_ref[hd + i] = kv4[:, (2 + i) * LANES:(3 + i) * LANES].astype(BF16)
    for hd in range(0, H, 2):
        q4 = _dot(cq, wqb_ref[0, :, hd // 2 * pair_cols:(hd // 2 + 1) * pair_cols])
        qn = [q4[:, i * LANES:(i + 1) * LANES] for i in range(2)]
        qpair = [q4[:, (2 + i) * LANES:(3 + i) * LANES] for i in range(2)]
        sq = jnp.concatenate([n * n + 0.5 * pr * pr for n, pr in zip(qn, qpair)], axis=1)
        r = lax.rsqrt(_dot(sq.astype(BF16), pair_ones) * inv_dim + EPS) * sm_scale
        for i in range(2):
            ri = r[:, i * LANES:(i + 1) * LANES]
            q_ref[hd + i, :, 0:LANES] = (qn[i] * ri * gq[0:1]).astype(BF16)
            q_ref[hd + i, :, LANES:HEAD_PAD] = (rope(qpair[i], gq) * ri).astype(BF16)


def _mla_proj(xs, mods, layer, j, g, p, cos_t, sin_t):
    x_specs, x_args = _row_specs(xs)
    row = lambda t: (t, 0)
    return pl.pallas_call(
        functools.partial(_mla_proj_kernel, n_x=len(x_args)),
        grid=(ALL_TILES,),
        in_specs=x_specs + [
            _mod_spec(layer),
            _layer_spec((1, D), layer),
            _layer_spec((D, Q_LORA), j),
            _layer_spec((1, Q_LORA), j),
            _layer_spec((Q_LORA, 2 * H * LANES), j),
            _layer_spec((D, KV_LORA + LANES), j),
            _layer_spec((1, KV_LORA), j),
            _layer_spec((KV_LORA, 2 * H * LANES), j),
            _layer_spec((8, LANES), j),
            _layer_spec((8, LANES), j),
            pl.BlockSpec((TM, LANES), row),
            pl.BlockSpec((TM, LANES), row),
        ],
        out_specs=[
            pl.BlockSpec((H, TM, HEAD_PAD), lambda t: (0, t, 0)),
            pl.BlockSpec((H, TM, HEAD_PAD), lambda t: (0, t, 0)),
            pl.BlockSpec((H, TM, V_DIM), lambda t: (0, t, 0)),
        ],
        out_shape=[
            jax.ShapeDtypeStruct((H, N_ALL, HEAD_PAD), BF16),
            jax.ShapeDtypeStruct((H, N_ALL, HEAD_PAD), BF16),
            jax.ShapeDtypeStruct((H, N_ALL, V_DIM), BF16),
        ],
        compiler_params=_params(1),
        name="mla_proj",
    )(*x_args, mods, g, p["wqa"], p["qan"], p["wqb"], p["wkva"], p["kvan"], p["wkvb"],
      p["gq"], p["gk"], cos_t, sin_t)


def _ones_column(rows):
    return (lax.broadcasted_iota(jnp.int32, (rows, LANES), 1) == 0).astype(BF16)


def _scores(q, k):
    return lax.dot_general(q, k, (((1,), (1,)), ((), ())), preferred_element_type=F32)


def _softmax_step(s, v, m, acc):
    m_new = jnp.max(s, axis=-1, keepdims=True)
    if m is not None:
        m_new = jnp.maximum(m, m_new)
    p = jnp.exp2(s - m_new).astype(BF16)
    pv = _dot(p, jnp.concatenate([v, _ones_column(v.shape[0])], axis=1))
    if m is None:
        return m_new, pv
    return m_new, jnp.exp2(m - m_new) * acc + pv


def _attn_lat_kernel(q_ref, kl_ref, kc_ref, vl_ref, vc_ref, o_ref):
    state = [(None, None)] * ATTN_HEADS
    for c in range(S // TK):
        rows = slice(c * TK, (c + 1) * TK)
        for i in range(ATTN_HEADS):
            state[i] = _softmax_step(_scores(q_ref[i], kl_ref[i, rows, :]), vl_ref[i, rows, :], *state[i])
    for i in range(ATTN_HEADS):
        _, acc = _softmax_step(_scores(q_ref[i], kc_ref[i]), vc_ref[i], *state[i])
        o_ref[:, i * V_DIM:(i + 1) * V_DIM] = (acc[:, 0:V_DIM] / acc[:, V_DIM:V_DIM + 1]).astype(BF16)


def _attn_lat(q, k, v):
    ctx_blk = N_LAT // CTX
    return pl.pallas_call(
        _attn_lat_kernel,
        grid=(B, H // ATTN_HEADS, S // TQ),
        in_specs=[
            pl.BlockSpec((ATTN_HEADS, TQ, HEAD_PAD), lambda b, h, i: (h, b * (S // TQ) + i, 0)),
            pl.BlockSpec((ATTN_HEADS, S, HEAD_PAD), lambda b, h, i: (h, b, 0)),
            pl.BlockSpec((ATTN_HEADS, CTX, HEAD_PAD), lambda b, h, i: (h, ctx_blk + b, 0)),
            pl.BlockSpec((ATTN_HEADS, S, V_DIM), lambda b, h, i: (h, b, 0)),
            pl.BlockSpec((ATTN_HEADS, CTX, V_DIM), lambda b, h, i: (h, ctx_blk + b, 0)),
        ],
        out_specs=pl.BlockSpec((TQ, ATTN_HEADS * V_DIM), lambda b, h, i: (b * (S // TQ) + i, h)),
        out_shape=jax.ShapeDtypeStruct((N_LAT, H * V_DIM), BF16),
        compiler_params=_params(3),
        name="attn_lat",
    )(q, k, k, v, v)


def _attn_ctx_kernel(q_ref, k_ref, v_ref, o_ref):
    for hd in range(H):
        _, acc = _softmax_step(_scores(q_ref[hd], k_ref[hd]), v_ref[hd], None, None)
        o_ref[:, hd * V_DIM:(hd + 1) * V_DIM] = (acc[:, 0:V_DIM] / acc[:, V_DIM:V_DIM + 1]).astype(BF16)


def _attn_ctx(q, k, v):
    blk = lambda b: (0, N_LAT // CTX + b, 0)
    return pl.pallas_call(
        _attn_ctx_kernel,
        grid=(B,),
        in_specs=[
            pl.BlockSpec((H, CTX, HEAD_PAD), blk),
            pl.BlockSpec((H, CTX, HEAD_PAD), blk),
            pl.BlockSpec((H, CTX, V_DIM), blk),
        ],
        out_specs=pl.BlockSpec((CTX, H * V_DIM), lambda b: (b, 0)),
        out_shape=jax.ShapeDtypeStruct((N_CTX, H * V_DIM), BF16),
        compiler_params=_params(1),
        name="attn_ctx",
    )(q, k, v)


def _gmlp_kernel(x_ref, mod_ref, g_ref, win_ref, lng_ref, lnb_ref, ws_ref, bs_ref, a_ref):
    mod = mod_ref[0]
    h = _modulate(x_ref[...], g_ref[0], mod[:, 0:D], mod[:, D:2 * D]).astype(BF16)
    sqrt_half = np.float32(np.sqrt(0.5))

    def gelu2(z):
        return z * (1.0 + lax.erf(z * sqrt_half))

    def lane_partial(a):
        acc = a[:, 0:LANES]
        for i in range(1, a.shape[1] // LANES):
            acc = acc + a[:, i * LANES:(i + 1) * LANES]
        return acc

    zks = []
    for c in range(GM_HALF // GM_VCHUNK):
        z = gelu2(_dot(h, win_ref[0, :, GM_HALF + c * GM_VCHUNK:GM_HALF + (c + 1) * GM_VCHUNK]))
        if c == 0:
            k = jnp.mean(z, axis=-1, keepdims=True)
        zk = z - k
        zks.append(zk)
        s1 = lane_partial(zk) if c == 0 else s1 + lane_partial(zk)
        s2 = lane_partial(zk * zk) if c == 0 else s2 + lane_partial(zk * zk)
    d = jnp.sum(s1, axis=-1, keepdims=True) * (1.0 / GM_HALF)
    var = jnp.sum(s2, axis=-1, keepdims=True) * (1.0 / GM_HALF) - d * d
    r = lax.rsqrt(var + 4.0 * EPS)
    vn = [((zk - d) * r).astype(BF16) for zk in zks]

    per_chunk = GM_VCHUNK // GM_GROUP_DIM
    for gi in range(GM_GROUPS):
        cols = slice(gi * GM_GROUP_DIM, (gi + 1) * GM_GROUP_DIM)
        u = gelu2(_dot(h, win_ref[0, :, cols]))
        ws = ws_ref[0, gi]
        bias = lnb_ref[0][:, cols] * jnp.sum(ws.astype(F32), axis=1, keepdims=True) + bs_ref[0, gi]
        gain = lng_ref[0][:, cols]
        vcols = slice((gi % per_chunk) * GM_GROUP_DIM, (gi % per_chunk + 1) * GM_GROUP_DIM)
        for c in range(TM // GM_CHUNK):
            rows = slice(c * GM_CHUNK, (c + 1) * GM_CHUNK)
            mixed = _dot(ws, vn[gi // per_chunk][rows, vcols]) * gain + bias
            a_ref[rows, cols] = (u[rows] * mixed).astype(BF16)


def _gmlp(xs, mods, layer, j, g, p, n_tiles):
    row = lambda t: (t, 0)
    return pl.pallas_call(
        _gmlp_kernel,
        grid=(n_tiles,),
        in_specs=[
            pl.BlockSpec((TM, D), row),
            _mod_spec(layer),
            _layer_spec((1, D), layer),
            _layer_spec((D, 2 * GM_HALF), j),
            _layer_spec((1, GM_HALF), j),
            _layer_spec((1, GM_HALF), j),
            _layer_spec((GM_GROUPS, GM_CHUNK, GM_CHUNK), j),
            _layer_spec((GM_GROUPS, GM_CHUNK, 1), j),
        ],
        out_specs=pl.BlockSpec((TM, GM_HALF), row),
        out_shape=jax.ShapeDtypeStruct((n_tiles * TM, GM_HALF), BF16),
        compiler_params=_params(1),
        name="gmlp_mix",
    )(xs, mods, g, p["w_in"], p["ln_g"], p["ln_b"], p["ws"], p["bs"])


def _out_ffn_kernel(*refs, n_x, n_a):
    x_refs, a_refs = refs[:n_x], refs[n_x:n_x + n_a]
    mod_ref, wo_ref, g_ref, w1_ref, w2_ref, o_ref = refs[n_x + n_a:]
    mod = mod_ref[0]
    x1 = _load_rows(x_refs) + mod[:, 2 * D:3 * D] * _dot(_load_rows(a_refs), wo_ref[0])
    h = _modulate(x1, g_ref[0], mod[:, 3 * D:4 * D], mod[:, 4 * D:5 * D]).astype(BF16)
    acc = jnp.zeros((TM, D), F32)
    for c in range(FFN_HIDDEN // FFN_CHUNK):
        cols = slice(c * FFN_CHUNK, (c + 1) * FFN_CHUNK)
        hid = jnp.maximum(_dot(h, w1_ref[0, :, cols]), 0.0)
        acc = acc + _dot((hid * hid).astype(BF16), w2_ref[0, cols, :])
    o_ref[...] = x1 + mod[:, 5 * D:6 * D] * acc


def _out_ffn(xs, a, mods, layer, j, wo, g, w1, w2, n_tiles):
    x_specs, x_args = _row_specs(xs)
    a_specs, a_args = _row_specs(a)
    row = lambda t: (t, 0)
    ka = a_args[0].shape[1]
    return pl.pallas_call(
        functools.partial(_out_ffn_kernel, n_x=len(x_args), n_a=len(a_args)),
        grid=(n_tiles,),
        in_specs=x_specs + a_specs + [
            _mod_spec(layer),
            _layer_spec((ka, D), j),
            _layer_spec((1, D), layer),
            _layer_spec((D, FFN_HIDDEN), layer),
            _layer_spec((FFN_HIDDEN, D), layer),
        ],
        out_specs=pl.BlockSpec((TM, D), row),
        out_shape=jax.ShapeDtypeStruct((n_tiles * TM, D), F32),
        compiler_params=_params(1),
        name="out_ffn",
    )(*x_args, *a_args, mods, wo, g, w1, w2)


def _rot_perm_sign():
    q = QK_ROPE // 4
    perm = np.concatenate([np.arange(q, 2 * q), np.arange(0, q), np.arange(3 * q, 4 * q), np.arange(2 * q, 3 * q)])
    sign = np.concatenate([-np.ones(q), np.ones(q), -np.ones(q), np.ones(q)]).astype(np.float32)
    return perm, sign


def _gain_rows(gain):
    perm, _ = _rot_perm_sign()
    n = gain.shape[0]
    pad = jnp.zeros((n, LANES - QK_ROPE), F32)
    rows = jnp.stack([gain[:, :QK_NOPE],
                      jnp.concatenate([gain[:, QK_NOPE:], pad], axis=1),
                      jnp.concatenate([gain[:, QK_NOPE:][:, perm], pad], axis=1)], axis=1)
    return jnp.concatenate([rows, jnp.zeros((n, 8 - 3, LANES), F32)], axis=1)


def _mla_params(wq_a, q_a_norm, wq_b, wkv_a, kv_a_norm, wkv_b, q_norm, k_norm):
    perm, sign = _rot_perm_sign()
    n = wq_a.shape[0]
    def by_head_pair(first, second):
        r = first.shape[1]
        parts = [t.reshape(n, r, H // 2, 2, LANES) for t in (first, second)]
        return jnp.stack(parts, axis=3).reshape(n, r, 2 * H * LANES)

    wq_b = wq_b.reshape(n, Q_LORA, H, QK_DIM)
    q_rope = wq_b[..., QK_NOPE:]
    wqb = by_head_pair(wq_b[..., :QK_NOPE], jnp.concatenate([q_rope, q_rope[..., perm] * sign], axis=-1))
    k_rope = wkv_a[..., KV_LORA:]
    wkva = jnp.concatenate([wkv_a, k_rope[..., perm] * sign], axis=-1)
    wkv_b = wkv_b.reshape(n, KV_LORA, H, QK_NOPE + V_DIM)
    wkvb = by_head_pair(wkv_b[..., :QK_NOPE], wkv_b[..., QK_NOPE:])
    return dict(wqa=wq_a.astype(BF16), qan=q_a_norm.reshape(n, 1, Q_LORA), wqb=wqb.astype(BF16),
                wkva=wkva.astype(BF16), kvan=kv_a_norm.reshape(n, 1, KV_LORA), wkvb=wkvb.astype(BF16),
                gq=_gain_rows(q_norm), gk=_gain_rows(k_norm))


def _rope_tables():
    row = jnp.repeat(jnp.arange(S // GRID_W, dtype=F32), GRID_W)
    col = jnp.tile(jnp.arange(GRID_W, dtype=F32), S // GRID_W)
    half = QK_ROPE // 2
    inv = ROPE_THETA ** (-jnp.arange(0, half, 2, dtype=F32) / half)
    ang_r = row[:, None] * inv[None, :]
    ang_c = col[:, None] * inv[None, :]
    ang = jnp.concatenate([ang_r, ang_r, ang_c, ang_c], axis=-1)
    pad = jnp.zeros((N_ALL, LANES - QK_ROPE), F32)
    cos = jnp.concatenate([jnp.tile(jnp.cos(ang), (B, 1)), jnp.ones((N_CTX, QK_ROPE), F32)])
    sin = jnp.concatenate([jnp.tile(jnp.sin(ang), (B, 1)), jnp.zeros((N_CTX, QK_ROPE), F32)])
    return jnp.concatenate([cos, pad], axis=1), jnp.concatenate([sin, pad], axis=1)


def kernel(x, c, ctx, c_ctx, ada_w, ada_b, norm_mix_g, norm_ffn_g, mla_wq_a, mla_q_a_norm, mla_wq_b,
           mla_wkv_a, mla_kv_a_norm, mla_wkv_b, mla_q_norm, mla_k_norm, mla_wo, gm_w_in, gm_ln_g,
           gm_ln_b, gm_ws, gm_bs, gm_w_out, ffn_w1, ffn_w2):
    c8 = jnp.concatenate([c, c_ctx[None], jnp.zeros((MOD_ROWS - B - 1, D), F32)])
    mods = _adaln(c8, ada_w, ada_b).reshape(DEPTH * MOD_ROWS, 1, N_MOD * D)
    cos_t, sin_t = _rope_tables()

    g_mix = norm_mix_g.reshape(DEPTH, 1, D)
    g_ffn = norm_ffn_g.reshape(DEPTH, 1, D)
    w1 = ffn_w1.astype(BF16)
    w2 = ffn_w2.astype(BF16)
    mla = _mla_params(mla_wq_a, mla_q_a_norm, mla_wq_b, mla_wkv_a, mla_kv_a_norm, mla_wkv_b,
                      mla_q_norm, mla_k_norm)
    mla_wo = mla_wo.astype(BF16)
    gm = dict(w_in=gm_w_in.astype(BF16), ln_g=gm_ln_g.reshape(-1, 1, GM_HALF),
              ln_b=gm_ln_b.reshape(-1, 1, GM_HALF), ws=(0.5 * gm_ws).astype(BF16),
              bs=(0.5 * gm_bs).reshape(-1, GM_GROUPS, GM_CHUNK, 1))
    gm_w_out = gm_w_out.astype(BF16)

    xs = (x.reshape(N_LAT, D), ctx.reshape(N_CTX, D))
    for i in range(DEPTH):
        j = i // 2
        if i % 2 == 0:
            q, k, v = _mla_proj(xs, mods, i, j, g_mix, mla, cos_t, sin_t)
            a = _attn_lat(q, k, v)
            ctx_live = i + 2 < DEPTH
            if ctx_live:
                a = (a, _attn_ctx(q, k, v))
            n_tiles = ALL_TILES if ctx_live else LAT_TILES
            xs = _out_ffn(xs, a, mods, i, j, mla_wo, g_ffn, w1, w2, n_tiles)
        else:
            n_tiles = LAT_TILES if i == DEPTH - 1 else ALL_TILES
            a = _gmlp(xs, mods, i, j, g_mix, gm, n_tiles)
            xs = _out_ffn(xs, a, mods, i, j, gm_w_out, g_ffn, w1, w2, n_tiles)
    return xs[:N_LAT].reshape(B, S, D)
```

```python
import functools

import jax
import jax.numpy as jnp
import numpy as np
from jax import lax
from jax.experimental import pallas as pl
from jax.experimental.pallas import tpu as pltpu

F32 = jnp.float32
BF16 = jnp.bfloat16

D = 1024
B = 4
S = 4096
DEPTH = 4
GRID_W = 64
CTX = 256
H = 8
QK_NOPE = 128
QK_ROPE = 64
V_DIM = 128
QK_DIM = QK_NOPE + QK_ROPE
Q_LORA = 384
KV_LORA = 256
ROPE_THETA = 10000.0
GM_CHUNK = 128
GM_GROUPS = 8
GM_HALF = 2 * D
GM_GROUP_DIM = GM_HALF // GM_GROUPS
FFN_HIDDEN = 4 * D
N_MOD = 6
EPS = 1e-6

N_LAT = B * S
N_CTX = B * CTX
N_ALL = N_LAT + N_CTX
LANES = 128
HEAD_PAD = 2 * LANES
TM = 512
TQ = 1024
TK = 512
ATTN_HEADS = 2
MOD_ROWS = 8
CTX_MOD_ROW = B
LAT_TILES = N_LAT // TM
ALL_TILES = N_ALL // TM
TILES_PER_BATCH = S // TM
ADA_TN = 1536
FFN_CHUNK = 1024
GM_VCHUNK = 512
FILL_BLOCK = (512, 1024)
VMEM_LIMIT = 56 * 1024 * 1024


def _params(grid_rank):
    return pltpu.CompilerParams(dimension_semantics=("arbitrary",) * grid_rank,
                                vmem_limit_bytes=VMEM_LIMIT)


def _mod_spec(layer):
    def index(t):
        return (layer * MOD_ROWS + jnp.where(t < LAT_TILES, t // TILES_PER_BATCH, CTX_MOD_ROW), 0, 0)
    return pl.BlockSpec((1, 1, N_MOD * D), index)


def _const_spec(shape):
    nd = len(shape)
    return pl.BlockSpec(shape, lambda *_: (0,) * nd, pipeline_mode=pl.Buffered(1))


def _layer_spec(shape, layer):
    nd = len(shape)
    return pl.BlockSpec((1,) + tuple(shape), lambda *_: (layer,) + (0,) * nd,
                        pipeline_mode=pl.Buffered(1))


def _row_specs(xs):
    if isinstance(xs, tuple):
        width = xs[0].shape[1]
        return [pl.BlockSpec((TM, width), lambda t: (jnp.minimum(t, LAT_TILES - 1), 0)),
                pl.BlockSpec((TM, width), lambda t: (jnp.maximum(t - LAT_TILES, 0), 0))], list(xs)
    return [pl.BlockSpec((TM, xs.shape[1]), lambda t: (t, 0))], [xs]


def _load_rows(x_refs):
    if len(x_refs) == 1:
        return x_refs[0][...]
    return jnp.where(pl.program_id(0) < LAT_TILES, x_refs[0][...], x_refs[1][...])


def _weight_blocks(w_hbm, layer, w_vmem):
    rows, cols = w_vmem.shape
    br, bc = FILL_BLOCK
    return [(w_hbm.at[layer, pl.ds(r, br), pl.ds(c, bc)], w_vmem.at[pl.ds(r, br), pl.ds(c, bc)])
            for r in range(0, rows, br) for c in range(0, cols, bc)]


def _fill_bf16_weights(blocks, stage_ref, sem_ref):
    @pl.when(pl.program_id(0) == 0)
    def _():
        def copy(i):
            return pltpu.make_async_copy(blocks[i][0], stage_ref.at[i % 2], sem_ref.at[i % 2])

        copy(0).start()
        for i in range(len(blocks)):
            if i + 1 < len(blocks):
                copy(i + 1).start()
            copy(i).wait()
            blocks[i][1][...] = stage_ref[i % 2].astype(BF16)


def _fill_scratch():
    return [pltpu.VMEM((2,) + FILL_BLOCK, F32), pltpu.SemaphoreType.DMA((2,))]


def _rms(x):
    return x * lax.rsqrt(jnp.mean(x * x, axis=-1, keepdims=True) + EPS)


def _modulate(x, g, shift, scale):
    return (_rms(x) * g) * (1.0 + scale) + shift


def _dot(a, b):
    return jnp.dot(a, b, preferred_element_type=F32)


def _adaln_kernel(c_ref, w_ref, b_ref, o_ref):
    c = c_ref[...]
    s = c * jax.nn.sigmoid(c)
    o_ref[0] = _dot(s.astype(BF16), w_ref[0].astype(BF16)) + b_ref[0]


def _adaln(c8, ada_w, ada_b):
    return pl.pallas_call(
        _adaln_kernel,
        grid=(DEPTH, N_MOD * D // ADA_TN),
        in_specs=[
            pl.BlockSpec((MOD_ROWS, D), lambda i, j: (0, 0)),
            pl.BlockSpec((1, D, ADA_TN), lambda i, j: (i, 0, j)),
            pl.BlockSpec((1, 1, ADA_TN), lambda i, j: (i, 0, j)),
        ],
        out_specs=pl.BlockSpec((1, MOD_ROWS, ADA_TN), lambda i, j: (i, 0, j)),
        out_shape=jax.ShapeDtypeStruct((DEPTH, MOD_ROWS, N_MOD * D), F32),
        compiler_params=_params(2),
        name="adaln",
    )(c8, ada_w, ada_b.reshape(DEPTH, 1, N_MOD * D))


def _mla_proj_kernel(*refs, n_x):
    x_refs = refs[:n_x]
    (mod_ref, g_ref, wqa_ref, qan_ref, wqb_ref, wkva_ref, kvan_ref, wkvb_ref, gq_ref, gk_ref,
     cos_ref, sin_ref, q_ref, k_ref, v_ref) = refs[n_x:]
    mod = mod_ref[0]
    x = _load_rows(x_refs)
    gq = gq_ref[0]
    gk = gk_ref[0]
    inv_dim = 1.0 / QK_DIM
    sm_scale = QK_DIM ** -0.5 * np.log2(np.e)
    lane_head = lax.broadcasted_iota(jnp.int32, (HEAD_PAD, HEAD_PAD), 0) // LANES
    pair_ones = (lane_head == lax.broadcasted_iota(jnp.int32, (HEAD_PAD, HEAD_PAD), 1) // LANES).astype(BF16)

    h = _modulate(x, g_ref[0], mod[:, 0:D], mod[:, D:2 * D]).astype(BF16)
    kva = _dot(h, wkva_ref[0])
    ckv = (_rms(kva[:, 0:KV_LORA]) * kvan_ref[0]).astype(BF16)
    cq = (_rms(_dot(h, wqa_ref[0])) * qan_ref[0]).astype(BF16)
    cos = cos_ref[...]
    sin = sin_ref[...]

    def rope(pair, gains):
        return pair * (cos * gains[1:2]) + pltpu.roll(pair, QK_ROPE, axis=1) * (sin * gains[2:3])

    kpair = kva[:, KV_LORA:KV_LORA + LANES]
    k_rope = rope(kpair, gk)
    k_pair_sq = 0.5 * kpair * kpair
    pair_cols = 4 * LANES
    for hd in range(0, H, 2):
        kv4 = _dot(ckv, wkvb_ref[0, :, hd // 2 * pair_cols:(hd // 2 + 1) * pair_cols])
        kn = [kv4[:, i * LANES:(i + 1) * LANES] for i in range(2)]
        sq = jnp.concatenate([n * n + k_pair_sq for n in kn], axis=1)
        r = lax.rsqrt(_dot(sq.astype(BF16), pair_ones) * inv_dim + EPS)
        for i in range(2):
            ri = r[:, i * LANES:(i + 1) * LANES]
            k_ref[hd + i, :, 0:LANES] = (kn[i] * ri * gk[0:1]).astype(BF16)
            k_ref[hd + i, :, LANES:HEAD_PAD] = (k_rope * ri).astype(BF16)
            v_ref[hd + i] = kv4[:, (2 + i) * LANES:(3 + i) * LANES].astype(BF16)
    for hd in range(0, H, 2):
        q4 = _dot(cq, wqb_ref[0, :, hd // 2 * pair_cols:(hd // 2 + 1) * pair_cols])
        qn = [q4[:, i * LANES:(i + 1) * LANES] for i in range(2)]
        qpair = [q4[:, (2 + i) * LANES:(3 + i) * LANES] for i in range(2)]
        sq = jnp.concatenate([n * n + 0.5 * pr * pr for n, pr in zip(qn, qpair)], axis=1)
        r = lax.rsqrt(_dot(sq.astype(BF16), pair_ones) * inv_dim + EPS) * sm_scale
        for i in range(2):
            ri = r[:, i * LANES:(i + 1) * LANES]
            q_ref[hd + i, :, 0:LANES] = (qn[i] * ri * gq[0:1]).astype(BF16)
            q_ref[hd + i, :, LANES:HEAD_PAD] = (rope(qpair[i], gq) * ri).astype(BF16)


def _mla_proj(xs, mods, layer, j, g, p, cos_t, sin_t):
    x_specs, x_args = _row_specs(xs)
    pos = lambda t: (jnp.where(t < LAT_TILES, t % TILES_PER_BATCH, TILES_PER_BATCH), 0)
    return pl.pallas_call(
        functools.partial(_mla_proj_kernel, n_x=len(x_args)),
        grid=(ALL_TILES,),
        in_specs=x_specs + [
            _mod_spec(layer),
            _layer_spec((1, D), layer),
            _layer_spec((D, Q_LORA), j),
            _layer_spec((1, Q_LORA), j),
            _layer_spec((Q_LORA, 2 * H * LANES), j),
            _layer_spec((D, KV_LORA + LANES), j),
            _layer_spec((1, KV_LORA), j),
            _layer_spec((KV_LORA, 2 * H * LANES), j),
            _layer_spec((8, LANES), j),
            _layer_spec((8, LANES), j),
            pl.BlockSpec((TM, LANES), pos),
            pl.BlockSpec((TM, LANES), pos),
        ],
        out_specs=[
            pl.BlockSpec((H, TM, HEAD_PAD), lambda t: (0, t, 0)),
            pl.BlockSpec((H, TM, HEAD_PAD), lambda t: (0, t, 0)),
            pl.BlockSpec((H, TM, V_DIM), lambda t: (0, t, 0)),
        ],
        out_shape=[
            jax.ShapeDtypeStruct((H, N_ALL, HEAD_PAD), BF16),
            jax.ShapeDtypeStruct((H, N_ALL, HEAD_PAD), BF16),
            jax.ShapeDtypeStruct((H, N_ALL, V_DIM), BF16),
        ],
        compiler_params=_params(1),
        name="mla_proj",
    )(*x_args, mods, g, p["wqa"], p["qan"], p["wqb"], p["wkva"], p["kvan"], p["wkvb"],
      p["gq"], p["gk"], cos_t, sin_t)


def _ones_column(rows):
    return (lax.broadcasted_iota(jnp.int32, (rows, LANES), 1) == 0).astype(BF16)


def _scores(q, k):
    return lax.dot_general(q, k, (((1,), (1,)), ((), ())), preferred_element_type=F32)


def _softmax_step(s, v, m, acc):
    m_new = jnp.max(s, axis=-1, keepdims=True)
    if m is not None:
        m_new = jnp.maximum(m, m_new)
    p = jnp.exp2(s - m_new).astype(BF16)
    pv = _dot(p, jnp.concatenate([v, _ones_column(v.shape[0])], axis=1))
    if m is None:
        return m_new, pv
    return m_new, jnp.exp2(m - m_new) * acc + pv


def _attn_lat_kernel(q_ref, kl_ref, kc_ref, vl_ref, vc_ref, o_ref):
    state = [(None, None)] * ATTN_HEADS
    for c in range(S // TK):
        rows = slice(c * TK, (c + 1) * TK)
        for i in range(ATTN_HEADS):
            state[i] = _softmax_step(_scores(q_ref[i], kl_ref[i, rows, :]), vl_ref[i, rows, :], *state[i])
    for i in range(ATTN_HEADS):
        _, acc = _softmax_step(_scores(q_ref[i], kc_ref[i]), vc_ref[i], *state[i])
        o_ref[:, i * V_DIM:(i + 1) * V_DIM] = (acc[:, 0:V_DIM] / acc[:, V_DIM:V_DIM + 1]).astype(BF16)


def _attn_lat(q, k, v):
    ctx_blk = N_LAT // CTX
    return pl.pallas_call(
        _attn_lat_kernel,
        grid=(B, H // ATTN_HEADS, S // TQ),
        in_specs=[
            pl.BlockSpec((ATTN_HEADS, TQ, HEAD_PAD), lambda b, h, i: (h, b * (S // TQ) + i, 0)),
            pl.BlockSpec((ATTN_HEADS, S, HEAD_PAD), lambda b, h, i: (h, b, 0)),
            pl.BlockSpec((ATTN_HEADS, CTX, HEAD_PAD), lambda b, h, i: (h, ctx_blk + b, 0)),
            pl.BlockSpec((ATTN_HEADS, S, V_DIM), lambda b, h, i: (h, b, 0)),
            pl.BlockSpec((ATTN_HEADS, CTX, V_DIM), lambda b, h, i: (h, ctx_blk + b, 0)),
        ],
        out_specs=pl.BlockSpec((TQ, ATTN_HEADS * V_DIM), lambda b, h, i: (b * (S // TQ) + i, h)),
        out_shape=jax.ShapeDtypeStruct((N_LAT, H * V_DIM), BF16),
        compiler_params=_params(3),
        name="attn_lat",
    )(q, k, k, v, v)


def _attn_ctx_kernel(q_ref, k_ref, v_ref, o_ref):
    for hd in range(H):
        _, acc = _softmax_step(_scores(q_ref[hd], k_ref[hd]), v_ref[hd], None, None)
        o_ref[:, hd * V_DIM:(hd + 1) * V_DIM] = (acc[:, 0:V_DIM] / acc[:, V_DIM:V_DIM + 1]).astype(BF16)


def _attn_ctx(q, k, v):
    blk = lambda b: (0, N_LAT // CTX + b, 0)
    return pl.pallas_call(
        _attn_ctx_kernel,
        grid=(B,),
        in_specs=[
            pl.BlockSpec((H, CTX, HEAD_PAD), blk),
            pl.BlockSpec((H, CTX, HEAD_PAD), blk),
            pl.BlockSpec((H, CTX, V_DIM), blk),
        ],
        out_specs=pl.BlockSpec((CTX, H * V_DIM), lambda b: (b, 0)),
        out_shape=jax.ShapeDtypeStruct((N_CTX, H * V_DIM), BF16),
        compiler_params=_params(1),
        name="attn_ctx",
    )(q, k, v)


def _gmlp_kernel(x_ref, mod_ref, g_ref, win_hbm, lng_ref, lnb_ref, ws_ref, bs_ref, a_ref,
                 win_ref, stage_ref, sem_ref, *, j):
    _fill_bf16_weights(_weight_blocks(win_hbm, j, win_ref), stage_ref, sem_ref)
    mod = mod_ref[0]
    h = _modulate(x_ref[...], g_ref[0], mod[:, 0:D], mod[:, D:2 * D]).astype(BF16)
    sqrt_half = np.float32(np.sqrt(0.5))

    def gelu2(z):
        return z * (1.0 + lax.erf(z * sqrt_half))

    def lane_partial(a):
        acc = a[:, 0:LANES]
        for i in range(1, a.shape[1] // LANES):
            acc = acc + a[:, i * LANES:(i + 1) * LANES]
        return acc

    zks = []
    for c in range(GM_HALF // GM_VCHUNK):
        z = gelu2(_dot(h, win_ref[:, GM_HALF + c * GM_VCHUNK:GM_HALF + (c + 1) * GM_VCHUNK]))
        if c == 0:
            k = jnp.mean(z, axis=-1, keepdims=True)
        zk = z - k
        zks.append(zk)
        s1 = lane_partial(zk) if c == 0 else s1 + lane_partial(zk)
        s2 = lane_partial(zk * zk) if c == 0 else s2 + lane_partial(zk * zk)
    d = jnp.sum(s1, axis=-1, keepdims=True) * (1.0 / GM_HALF)
    var = jnp.sum(s2, axis=-1, keepdims=True) * (1.0 / GM_HALF) - d * d
    r = lax.rsqrt(var + 4.0 * EPS)
    vn = [((zk - d) * r).astype(BF16) for zk in zks]

    per_chunk = GM_VCHUNK // GM_GROUP_DIM
    for gi in range(GM_GROUPS):
        cols = slice(gi * GM_GROUP_DIM, (gi + 1) * GM_GROUP_DIM)
        u = gelu2(_dot(h, win_ref[:, cols]))
        ws = ws_ref[0, gi]
        bias = lnb_ref[0][:, cols] * jnp.sum(ws.astype(F32), axis=1, keepdims=True) + bs_ref[0, gi]
        gain = lng_ref[0][:, cols]
        vcols = slice((gi % per_chunk) * GM_GROUP_DIM, (gi % per_chunk + 1) * GM_GROUP_DIM)
        for c in range(TM // GM_CHUNK):
            rows = slice(c * GM_CHUNK, (c + 1) * GM_CHUNK)
            mixed = _dot(ws, vn[gi // per_chunk][rows, vcols]) * gain + bias
            a_ref[rows, cols] = (u[rows] * mixed).astype(BF16)


def _gmlp(xs, mods, layer, j, g, p, n_tiles):
    row = lambda t: (t, 0)
    return pl.pallas_call(
        functools.partial(_gmlp_kernel, j=j),
        grid=(n_tiles,),
        in_specs=[
            pl.BlockSpec((TM, D), row),
            _mod_spec(layer),
            _layer_spec((1, D), layer),
            pl.BlockSpec(memory_space=pl.ANY),
            _layer_spec((1, GM_HALF), j),
            _layer_spec((1, GM_HALF), j),
            _layer_spec((GM_GROUPS, GM_CHUNK, GM_CHUNK), j),
            _layer_spec((GM_GROUPS, GM_CHUNK, 1), j),
        ],
        out_specs=pl.BlockSpec((TM, GM_HALF), row),
        out_shape=jax.ShapeDtypeStruct((n_tiles * TM, GM_HALF), BF16),
        scratch_shapes=[pltpu.VMEM((D, 2 * GM_HALF), BF16)] + _fill_scratch(),
        compiler_params=_params(1),
        name="gmlp_mix",
    )(xs, mods, g, p["w_in"], p["ln_g"], p["ln_b"], p["ws"], p["bs"])


def _out_ffn_kernel(*refs, n_x, n_a, layer, j):
    x_refs, a_refs = refs[:n_x], refs[n_x:n_x + n_a]
    mod_ref, g_ref, wo_hbm, w1_hbm, w2_hbm, o_ref, wo_ref, w1_ref, w2_ref, stage_ref, sem_ref = refs[n_x + n_a:]
    _fill_bf16_weights(_weight_blocks(wo_hbm, j, wo_ref) + _weight_blocks(w1_hbm, layer, w1_ref)
                       + _weight_blocks(w2_hbm, layer, w2_ref), stage_ref, sem_ref)
    mod = mod_ref[0]
    x1 = _load_rows(x_refs) + mod[:, 2 * D:3 * D] * _dot(_load_rows(a_refs), wo_ref[...])
    h = _modulate(x1, g_ref[0], mod[:, 3 * D:4 * D], mod[:, 4 * D:5 * D]).astype(BF16)
    acc = jnp.zeros((TM, D), F32)
    for c in range(FFN_HIDDEN // FFN_CHUNK):
        cols = slice(c * FFN_CHUNK, (c + 1) * FFN_CHUNK)
        hid = jnp.maximum(_dot(h, w1_ref[:, cols]), 0.0)
        acc = acc + _dot((hid * hid).astype(BF16), w2_ref[cols, :])
    o_ref[...] = x1 + mod[:, 5 * D:6 * D] * acc


def _out_ffn(xs, a, mods, layer, j, wo, g, w1, w2, n_tiles):
    x_specs, x_args = _row_specs(xs)
    a_specs, a_args = _row_specs(a)
    row = lambda t: (t, 0)
    ka = a_args[0].shape[1]
    hbm = pl.BlockSpec(memory_space=pl.ANY)
    return pl.pallas_call(
        functools.partial(_out_ffn_kernel, n_x=len(x_args), n_a=len(a_args), layer=layer, j=j),
        grid=(n_tiles,),
        in_specs=x_specs + a_specs + [_mod_spec(layer), _layer_spec((1, D), layer), hbm, hbm, hbm],
        out_specs=pl.BlockSpec((TM, D), row),
        out_shape=jax.ShapeDtypeStruct((n_tiles * TM, D), F32),
        scratch_shapes=[pltpu.VMEM((ka, D), BF16), pltpu.VMEM((D, FFN_HIDDEN), BF16),
                        pltpu.VMEM((FFN_HIDDEN, D), BF16)] + _fill_scratch(),
        compiler_params=_params(1),
        name="out_ffn",
    )(*x_args, *a_args, mods, g, wo, w1, w2)


def _rot_perm_sign():
    q = QK_ROPE // 4
    perm = np.concatenate([np.arange(q, 2 * q), np.arange(0, q), np.arange(3 * q, 4 * q), np.arange(2 * q, 3 * q)])
    sign = np.concatenate([-np.ones(q), np.ones(q), -np.ones(q), np.ones(q)]).astype(np.float32)
    return perm, sign


def _gain_rows(gain):
    perm, _ = _rot_perm_sign()
    n = gain.shape[0]
    pad = jnp.zeros((n, LANES - QK_ROPE), F32)
    rows = jnp.stack([gain[:, :QK_NOPE],
                      jnp.concatenate([gain[:, QK_NOPE:], pad], axis=1),
                      jnp.concatenate([gain[:, QK_NOPE:][:, perm], pad], axis=1)], axis=1)
    return jnp.concatenate([rows, jnp.zeros((n, 8 - 3, LANES), F32)], axis=1)


def _mla_params(wq_a, q_a_norm, wq_b, wkv_a, kv_a_norm, wkv_b, q_norm, k_norm):
    perm, sign = _rot_perm_sign()
    n = wq_a.shape[0]
    def by_head_pair(first, second):
        r = first.shape[1]
        parts = [t.reshape(n, r, H // 2, 2, LANES) for t in (first, second)]
        return jnp.stack(parts, axis=3).reshape(n, r, 2 * H * LANES)

    wq_b = wq_b.reshape(n, Q_LORA, H, QK_DIM)
    q_rope = wq_b[..., QK_NOPE:]
    wqb = by_head_pair(wq_b[..., :QK_NOPE], jnp.concatenate([q_rope, q_rope[..., perm] * sign], axis=-1))
    k_rope = wkv_a[..., KV_LORA:]
    wkva = jnp.concatenate([wkv_a, k_rope[..., perm] * sign], axis=-1)
    wkv_b = wkv_b.reshape(n, KV_LORA, H, QK_NOPE + V_DIM)
    wkvb = by_head_pair(wkv_b[..., :QK_NOPE], wkv_b[..., QK_NOPE:])
    return dict(wqa=wq_a.astype(BF16), qan=q_a_norm.reshape(n, 1, Q_LORA), wqb=wqb.astype(BF16),
                wkva=wkva.astype(BF16), kvan=kv_a_norm.reshape(n, 1, KV_LORA), wkvb=wkvb.astype(BF16),
                gq=_gain_rows(q_norm), gk=_gain_rows(k_norm))


def _rope_tables():
    row = np.repeat(np.arange(S // GRID_W, dtype=np.float32), GRID_W)
    col = np.tile(np.arange(GRID_W, dtype=np.float32), S // GRID_W)
    half = QK_ROPE // 2
    inv = (np.float32(ROPE_THETA) ** (-np.arange(0, half, 2, dtype=np.float32) / np.float32(half))).astype(np.float32)
    ang_r = row[:, None] * inv[None, :]
    ang_c = col[:, None] * inv[None, :]
    ang = np.concatenate([ang_r, ang_r, ang_c, ang_c], axis=-1)
    cos = np.zeros((S + TM, LANES), np.float32)
    sin = np.zeros((S + TM, LANES), np.float32)
    cos[:S, :QK_ROPE] = np.cos(ang)
    sin[:S, :QK_ROPE] = np.sin(ang)
    cos[S:, :QK_ROPE] = 1.0
    return jnp.asarray(cos), jnp.asarray(sin)


def kernel(x, c, ctx, c_ctx, ada_w, ada_b, norm_mix_g, norm_ffn_g, mla_wq_a, mla_q_a_norm, mla_wq_b,
           mla_wkv_a, mla_kv_a_norm, mla_wkv_b, mla_q_norm, mla_k_norm, mla_wo, gm_w_in, gm_ln_g,
           gm_ln_b, gm_ws, gm_bs, gm_w_out, ffn_w1, ffn_w2):
    c8 = jnp.concatenate([c, c_ctx[None], jnp.zeros((MOD_ROWS - B - 1, D), F32)])
    mods = _adaln(c8, ada_w, ada_b).reshape(DEPTH * MOD_ROWS, 1, N_MOD * D)
    cos_t, sin_t = _rope_tables()

    g_mix = norm_mix_g.reshape(DEPTH, 1, D)
    g_ffn = norm_ffn_g.reshape(DEPTH, 1, D)
    w1, w2 = ffn_w1, ffn_w2
    mla = _mla_params(mla_wq_a, mla_q_a_norm, mla_wq_b, mla_wkv_a, mla_kv_a_norm, mla_wkv_b,
                      mla_q_norm, mla_k_norm)
    gm = dict(w_in=gm_w_in, ln_g=gm_ln_g.reshape(-1, 1, GM_HALF),
              ln_b=gm_ln_b.reshape(-1, 1, GM_HALF), ws=(0.5 * gm_ws).astype(BF16),
              bs=(0.5 * gm_bs).reshape(-1, GM_GROUPS, GM_CHUNK, 1))

    xs = (x.reshape(N_LAT, D), ctx.reshape(N_CTX, D))
    for i in range(DEPTH):
        j = i // 2
        if i % 2 == 0:
            q, k, v = _mla_proj(xs, mods, i, j, g_mix, mla, cos_t, sin_t)
            a = _attn_lat(q, k, v)
            ctx_live = i + 2 < DEPTH
            if ctx_live:
                a = (a, _attn_ctx(q, k, v))
            n_tiles = ALL_TILES if ctx_live else LAT_TILES
            xs = _out_ffn(xs, a, mods, i, j, mla_wo, g_ffn, w1, w2, n_tiles)
        else:
            n_tiles = LAT_TILES if i == DEPTH - 1 else ALL_TILES
            a = _gmlp(xs, mods, i, j, g_mix, gm, n_tiles)
            xs = _out_ffn(xs, a, mods, i, j, gm_w_out, g_ffn, w1, w2, n_tiles)
    return xs[:N_LAT].reshape(B, S, D)
```

```python
import functools

import jax
import jax.numpy as jnp
import numpy as np
from jax import lax
from jax.experimental import pallas as pl
from jax.experimental.pallas import tpu as pltpu

F32 = jnp.float32
BF16 = jnp.bfloat16

D = 1024
B = 4
S = 4096
DEPTH = 4
GRID_W = 64
CTX = 256
H = 8
QK_NOPE = 128
QK_ROPE = 64
V_DIM = 128
QK_DIM = QK_NOPE + QK_ROPE
Q_LORA = 384
KV_LORA = 256
ROPE_THETA = 10000.0
GM_CHUNK = 128
GM_GROUPS = 8
GM_HALF = 2 * D
GM_GROUP_DIM = GM_HALF // GM_GROUPS
FFN_HIDDEN = 4 * D
N_MOD = 6
EPS = 1e-6

N_LAT = B * S
N_CTX = B * CTX
N_ALL = N_LAT + N_CTX
LANES = 128
HEAD_PAD = 2 * LANES
TM = 512
TQ = 1024
TK = 512
ATTN_HEADS = 2
PLAIN_SCORE_LIMIT = 64.0
MOD_ROWS = 8
CTX_MOD_ROW = B
LAT_TILES = N_LAT // TM
ALL_TILES = N_ALL // TM
TILES_PER_BATCH = S // TM
ADA_TN = 1536
FFN_CHUNK = 1024
GM_VCHUNK = 512
FILL_BLOCK = (512, 1024)
VMEM_LIMIT = 56 * 1024 * 1024


def _params(grid_rank):
    return pltpu.CompilerParams(dimension_semantics=("arbitrary",) * grid_rank,
                                vmem_limit_bytes=VMEM_LIMIT)


def _mod_spec(layer):
    def index(t):
        return (layer * MOD_ROWS + jnp.where(t < LAT_TILES, t // TILES_PER_BATCH, CTX_MOD_ROW), 0, 0)
    return pl.BlockSpec((1, 1, N_MOD * D), index)


def _const_spec(shape):
    nd = len(shape)
    return pl.BlockSpec(shape, lambda *_: (0,) * nd, pipeline_mode=pl.Buffered(1))


def _layer_spec(shape, layer):
    nd = len(shape)
    return pl.BlockSpec((1,) + tuple(shape), lambda *_: (layer,) + (0,) * nd,
                        pipeline_mode=pl.Buffered(1))


def _row_specs(xs):
    if isinstance(xs, tuple):
        width = xs[0].shape[1]
        return [pl.BlockSpec((TM, width), lambda t: (jnp.minimum(t, LAT_TILES - 1), 0)),
                pl.BlockSpec((TM, width), lambda t: (jnp.maximum(t - LAT_TILES, 0), 0))], list(xs)
    return [pl.BlockSpec((TM, xs.shape[1]), lambda t: (t, 0))], [xs]


def _load_rows(x_refs):
    if len(x_refs) == 1:
        return x_refs[0][...]
    return jnp.where(pl.program_id(0) < LAT_TILES, x_refs[0][...], x_refs[1][...])


def _weight_blocks(w_hbm, layer, w_vmem):
    rows, cols = w_vmem.shape
    br, bc = FILL_BLOCK
    return [(w_hbm.at[layer, pl.ds(r, br), pl.ds(c, bc)], w_vmem.at[pl.ds(r, br), pl.ds(c, bc)])
            for r in range(0, rows, br) for c in range(0, cols, bc)]


def _fill_bf16_weights(blocks, stage_ref, sem_ref):
    @pl.when(pl.program_id(0) == 0)
    def _():
        def copy(i):
            return pltpu.make_async_copy(blocks[i][0], stage_ref.at[i % 2], sem_ref.at[i % 2])

        copy(0).start()
        for i in range(len(blocks)):
            if i + 1 < len(blocks):
                copy(i + 1).start()
            copy(i).wait()
            blocks[i][1][...] = stage_ref[i % 2].astype(BF16)


def _fill_scratch():
    return [pltpu.VMEM((2,) + FILL_BLOCK, F32), pltpu.SemaphoreType.DMA((2,))]


def _rms(x):
    return x * lax.rsqrt(jnp.mean(x * x, axis=-1, keepdims=True) + EPS)


def _modulate(x, g, shift, scale):
    return (_rms(x) * g) * (1.0 + scale) + shift


def _dot(a, b):
    return jnp.dot(a, b, preferred_element_type=F32)


def _adaln_kernel(c_ref, w_ref, b_ref, o_ref):
    c = c_ref[...]
    s = c * jax.nn.sigmoid(c)
    o_ref[0] = _dot(s.astype(BF16), w_ref[0].astype(BF16)) + b_ref[0]


def _adaln(c8, ada_w, ada_b):
    return pl.pallas_call(
        _adaln_kernel,
        grid=(DEPTH, N_MOD * D // ADA_TN),
        in_specs=[
            pl.BlockSpec((MOD_ROWS, D), lambda i, j: (0, 0)),
            pl.BlockSpec((1, D, ADA_TN), lambda i, j: (i, 0, j)),
            pl.BlockSpec((1, 1, ADA_TN), lambda i, j: (i, 0, j)),
        ],
        out_specs=pl.BlockSpec((1, MOD_ROWS, ADA_TN), lambda i, j: (i, 0, j)),
        out_shape=jax.ShapeDtypeStruct((DEPTH, MOD_ROWS, N_MOD * D), F32),
        compiler_params=_params(2),
        name="adaln",
    )(c8, ada_w, ada_b.reshape(DEPTH, 1, N_MOD * D))


def _mla_proj_kernel(*refs, n_x):
    x_refs = refs[:n_x]
    (mod_ref, g_ref, wqa_ref, qan_ref, wqb_ref, wkva_ref, kvan_ref, wkvb_ref, gq_ref, gk_ref,
     cos_ref, sin_ref, q_ref, k_ref, v_ref) = refs[n_x:]
    mod = mod_ref[0]
    x = _load_rows(x_refs)
    gq = gq_ref[0]
    gk = gk_ref[0]
    inv_dim = 1.0 / QK_DIM
    sm_scale = QK_DIM ** -0.5 * np.log2(np.e)
    lane_head = lax.broadcasted_iota(jnp.int32, (HEAD_PAD, HEAD_PAD), 0) // LANES
    pair_ones = (lane_head == lax.broadcasted_iota(jnp.int32, (HEAD_PAD, HEAD_PAD), 1) // LANES).astype(BF16)

    h = _modulate(x, g_ref[0], mod[:, 0:D], mod[:, D:2 * D]).astype(BF16)
    kva = _dot(h, wkva_ref[0])
    ckv = (_rms(kva[:, 0:KV_LORA]) * kvan_ref[0]).astype(BF16)
    cq = (_rms(_dot(h, wqa_ref[0])) * qan_ref[0]).astype(BF16)
    cos = cos_ref[...]
    sin = sin_ref[...]

    def rope(pair, gains):
        return pair * (cos * gains[1:2]) + pltpu.roll(pair, QK_ROPE, axis=1) * (sin * gains[2:3])

    kpair = kva[:, KV_LORA:KV_LORA + LANES]
    k_rope = rope(kpair, gk)
    k_pair_sq = 0.5 * kpair * kpair
    pair_cols = 4 * LANES
    for hd in range(0, H, 2):
        kv4 = _dot(ckv, wkvb_ref[0, :, hd // 2 * pair_cols:(hd // 2 + 1) * pair_cols])
        kn = [kv4[:, i * LANES:(i + 1) * LANES] for i in range(2)]
        sq = jnp.concatenate([n * n + k_pair_sq for n in kn], axis=1)
        r = lax.rsqrt(_dot(sq.astype(BF16), pair_ones) * inv_dim + EPS)
        for i in range(2):
            ri = r[:, i * LANES:(i + 1) * LANES]
            k_ref[hd + i, :, 0:LANES] = (kn[i] * ri * gk[0:1]).astype(BF16)
            k_ref[hd + i, :, LANES:HEAD_PAD] = (k_rope * ri).astype(BF16)
            v_ref[hd + i] = kv4[:, (2 + i) * LANES:(3 + i) * LANES].astype(BF16)
    for hd in range(0, H, 2):
        q4 = _dot(cq, wqb_ref[0, :, hd // 2 * pair_cols:(hd // 2 + 1) * pair_cols])
        qn = [q4[:, i * LANES:(i + 1) * LANES] for i in range(2)]
        qpair = [q4[:, (2 + i) * LANES:(3 + i) * LANES] for i in range(2)]
        sq = jnp.concatenate([n * n + 0.5 * pr * pr for n, pr in zip(qn, qpair)], axis=1)
        r = lax.rsqrt(_dot(sq.astype(BF16), pair_ones) * inv_dim + EPS) * sm_scale
        for i in range(2):
            ri = r[:, i * LANES:(i + 1) * LANES]
            q_ref[hd + i, :, 0:LANES] = (qn[i] * ri * gq[0:1]).astype(BF16)
            q_ref[hd + i, :, LANES:HEAD_PAD] = (rope(qpair[i], gq) * ri).astype(BF16)


def _mla_proj(xs, mods, layer, j, g, p, cos_t, sin_t):
    x_specs, x_args = _row_specs(xs)
    pos = lambda t: (jnp.where(t < LAT_TILES, t % TILES_PER_BATCH, TILES_PER_BATCH), 0)
    return pl.pallas_call(
        functools.partial(_mla_proj_kernel, n_x=len(x_args)),
        grid=(ALL_TILES,),
        in_specs=x_specs + [
            _mod_spec(layer),
            _layer_spec((1, D), layer),
            _layer_spec((D, Q_LORA), j),
            _layer_spec((1, Q_LORA), j),
            _layer_spec((Q_LORA, 2 * H * LANES), j),
            _layer_spec((D, KV_LORA + LANES), j),
            _layer_spec((1, KV_LORA), j),
            _layer_spec((KV_LORA, 2 * H * LANES), j),
            _layer_spec((8, LANES), j),
            _layer_spec((8, LANES), j),
            pl.BlockSpec((TM, LANES), pos),
            pl.BlockSpec((TM, LANES), pos),
        ],
        out_specs=[
            pl.BlockSpec((H, TM, HEAD_PAD), lambda t: (0, t, 0)),
            pl.BlockSpec((H, TM, HEAD_PAD), lambda t: (0, t, 0)),
            pl.BlockSpec((H, TM, V_DIM), lambda t: (0, t, 0)),
        ],
        out_shape=[
            jax.ShapeDtypeStruct((H, N_ALL, HEAD_PAD), BF16),
            jax.ShapeDtypeStruct((H, N_ALL, HEAD_PAD), BF16),
            jax.ShapeDtypeStruct((H, N_ALL, V_DIM), BF16),
        ],
        compiler_params=_params(1),
        name="mla_proj",
    )(*x_args, mods, g, p["wqa"], p["qan"], p["wqb"], p["wkva"], p["kvan"], p["wkvb"],
      p["gq"], p["gk"], cos_t, sin_t)


def _ones_column(rows):
    return (lax.broadcasted_iota(jnp.int32, (rows, LANES), 1) == 0).astype(BF16)


def _scores(q, k):
    return lax.dot_general(q, k, (((1,), (1,)), ((), ())), preferred_element_type=F32)


def _softmax_step(s, v, m, acc):
    m_new = jnp.max(s, axis=-1, keepdims=True)
    if m is not None:
        m_new = jnp.maximum(m, m_new)
    p = jnp.exp2(s - m_new).astype(BF16)
    pv = _dot(p, jnp.concatenate([v, _ones_column(v.shape[0])], axis=1))
    if m is None:
        return m_new, pv
    return m_new, jnp.exp2(m - m_new) * acc + pv


def _plain_step(s, v, m, acc):
    del m
    pv = _dot(jnp.exp2(s).astype(BF16), jnp.concatenate([v, _ones_column(v.shape[0])], axis=1))
    return None, pv if acc is None else acc + pv


def _attn_heads(step, q_ref, kl_ref, kc_ref, vl_ref, vc_ref, o_ref):
    state = [(None, None)] * ATTN_HEADS
    for c in range(S // TK):
        rows = slice(c * TK, (c + 1) * TK)
        for i in range(ATTN_HEADS):
            state[i] = step(_scores(q_ref[i], kl_ref[i, rows, :]), vl_ref[i, rows, :], *state[i])
    for i in range(ATTN_HEADS):
        _, acc = step(_scores(q_ref[i], kc_ref[i]), vc_ref[i], *state[i])
        o_ref[:, i * V_DIM:(i + 1) * V_DIM] = (acc[:, 0:V_DIM] / acc[:, V_DIM:V_DIM + 1]).astype(BF16)


def _attn_lat_kernel(plain_ref, *refs):
    @pl.when(plain_ref[0] != 0)
    def _():
        _attn_heads(_plain_step, *refs)

    @pl.when(plain_ref[0] == 0)
    def _():
        _attn_heads(_softmax_step, *refs)


def _plain_softmax_ok(q_gain, k_gain):
    bound = QK_DIM * jnp.max(jnp.abs(q_gain)) * jnp.max(jnp.abs(k_gain)) * (QK_DIM ** -0.5 * np.log2(np.e)) * 1.02
    return (bound <= PLAIN_SCORE_LIMIT).astype(jnp.int32).reshape(1)


def _attn_lat(plain_ok, q, k, v):
    ctx_blk = N_LAT // CTX
    return pl.pallas_call(
        _attn_lat_kernel,
        grid=(B, H // ATTN_HEADS, S // TQ),
        in_specs=[
            pl.BlockSpec(memory_space=pltpu.SMEM),
            pl.BlockSpec((ATTN_HEADS, TQ, HEAD_PAD), lambda b, h, i: (h, b * (S // TQ) + i, 0)),
            pl.BlockSpec((ATTN_HEADS, S, HEAD_PAD), lambda b, h, i: (h, b, 0)),
            pl.BlockSpec((ATTN_HEADS, CTX, HEAD_PAD), lambda b, h, i: (h, ctx_blk + b, 0)),
            pl.BlockSpec((ATTN_HEADS, S, V_DIM), lambda b, h, i: (h, b, 0)),
            pl.BlockSpec((ATTN_HEADS, CTX, V_DIM), lambda b, h, i: (h, ctx_blk + b, 0)),
        ],
        out_specs=pl.BlockSpec((TQ, ATTN_HEADS * V_DIM), lambda b, h, i: (b * (S // TQ) + i, h)),
        out_shape=jax.ShapeDtypeStruct((N_LAT, H * V_DIM), BF16),
        compiler_params=_params(3),
        name="attn_lat",
    )(plain_ok, q, k, k, v, v)


def _attn_ctx_kernel(q_ref, k_ref, v_ref, o_ref):
    for hd in range(H):
        _, acc = _softmax_step(_scores(q_ref[hd], k_ref[hd]), v_ref[hd], None, None)
        o_ref[:, hd * V_DIM:(hd + 1) * V_DIM] = (acc[:, 0:V_DIM] / acc[:, V_DIM:V_DIM + 1]).astype(BF16)


def _attn_ctx(q, k, v):
    blk = lambda b: (0, N_LAT // CTX + b, 0)
    return pl.pallas_call(
        _attn_ctx_kernel,
        grid=(B,),
        in_specs=[
            pl.BlockSpec((H, CTX, HEAD_PAD), blk),
            pl.BlockSpec((H, CTX, HEAD_PAD), blk),
            pl.BlockSpec((H, CTX, V_DIM), blk),
        ],
        out_specs=pl.BlockSpec((CTX, H * V_DIM), lambda b: (b, 0)),
        out_shape=jax.ShapeDtypeStruct((N_CTX, H * V_DIM), BF16),
        compiler_params=_params(1),
        name="attn_ctx",
    )(q, k, v)


def _gmlp_kernel(x_ref, mod_ref, g_ref, win_hbm, lng_ref, lnb_ref, ws_ref, bs_ref, a_ref,
                 win_ref, stage_ref, sem_ref, *, j):
    _fill_bf16_weights(_weight_blocks(win_hbm, j, win_ref), stage_ref, sem_ref)
    mod = mod_ref[0]
    h = _modulate(x_ref[...], g_ref[0], mod[:, 0:D], mod[:, D:2 * D]).astype(BF16)
    sqrt_half = np.float32(np.sqrt(0.5))

    def gelu2(z):
        return z * (1.0 + lax.erf(z * sqrt_half))

    def lane_partial(a):
        acc = a[:, 0:LANES]
        for i in range(1, a.shape[1] // LANES):
            acc = acc + a[:, i * LANES:(i + 1) * LANES]
        return acc

    zks = []
    for c in range(GM_HALF // GM_VCHUNK):
        z = gelu2(_dot(h, win_ref[:, GM_HALF + c * GM_VCHUNK:GM_HALF + (c + 1) * GM_VCHUNK]))
        if c == 0:
            k = jnp.mean(z, axis=-1, keepdims=True)
        zk = z - k
        zks.append(zk)
        s1 = lane_partial(zk) if c == 0 else s1 + lane_partial(zk)
        s2 = lane_partial(zk * zk) if c == 0 else s2 + lane_partial(zk * zk)
    d = jnp.sum(s1, axis=-1, keepdims=True) * (1.0 / GM_HALF)
    var = jnp.sum(s2, axis=-1, keepdims=True) * (1.0 / GM_HALF) - d * d
    r = lax.rsqrt(var + 4.0 * EPS)
    vn = [((zk - d) * r).astype(BF16) for zk in zks]

    per_chunk = GM_VCHUNK // GM_GROUP_DIM
    for gi in range(GM_GROUPS):
        cols = slice(gi * GM_GROUP_DIM, (gi + 1) * GM_GROUP_DIM)
        u = gelu2(_dot(h, win_ref[:, cols]))
        ws = ws_ref[0, gi]
        bias = lnb_ref[0][:, cols] * jnp.sum(ws.astype(F32), axis=1, keepdims=True) + bs_ref[0, gi]
        gain = lng_ref[0][:, cols]
        vcols = slice((gi % per_chunk) * GM_GROUP_DIM, (gi % per_chunk + 1) * GM_GROUP_DIM)
        for c in range(TM // GM_CHUNK):
            rows = slice(c * GM_CHUNK, (c + 1) * GM_CHUNK)
            mixed = _dot(ws, vn[gi // per_chunk][rows, vcols]) * gain + bias
            a_ref[rows, cols] = (u[rows] * mixed).astype(BF16)


def _gmlp(xs, mods, layer, j, g, p, n_tiles):
    row = lambda t: (t, 0)
    return pl.pallas_call(
        functools.partial(_gmlp_kernel, j=j),
        grid=(n_tiles,),
        in_specs=[
            pl.BlockSpec((TM, D), row),
            _mod_spec(layer),
            _layer_spec((1, D), layer),
            pl.BlockSpec(memory_space=pl.ANY),
            _layer_spec((1, GM_HALF), j),
            _layer_spec((1, GM_HALF), j),
            _layer_spec((GM_GROUPS, GM_CHUNK, GM_CHUNK), j),
            _layer_spec((GM_GROUPS, GM_CHUNK, 1), j),
        ],
        out_specs=pl.BlockSpec((TM, GM_HALF), row),
        out_shape=jax.ShapeDtypeStruct((n_tiles * TM, GM_HALF), BF16),
        scratch_shapes=[pltpu.VMEM((D, 2 * GM_HALF), BF16)] + _fill_scratch(),
        compiler_params=_params(1),
        name="gmlp_mix",
    )(xs, mods, g, p["w_in"], p["ln_g"], p["ln_b"], p["ws"], p["bs"])


def _out_ffn_kernel(*refs, n_x, n_a, layer, j):
    x_refs, a_refs = refs[:n_x], refs[n_x:n_x + n_a]
    mod_ref, g_ref, wo_hbm, w1_hbm, w2_hbm, o_ref, wo_ref, w1_ref, w2_ref, stage_ref, sem_ref = refs[n_x + n_a:]
    _fill_bf16_weights(_weight_blocks(wo_hbm, j, wo_ref) + _weight_blocks(w1_hbm, layer, w1_ref)
                       + _weight_blocks(w2_hbm, layer, w2_ref), stage_ref, sem_ref)
    mod = mod_ref[0]
    x1 = _load_rows(x_refs) + mod[:, 2 * D:3 * D] * _dot(_load_rows(a_refs), wo_ref[...])
    h = _modulate(x1, g_ref[0], mod[:, 3 * D:4 * D], mod[:, 4 * D:5 * D]).astype(BF16)
    acc = jnp.zeros((TM, D), F32)
    for c in range(FFN_HIDDEN // FFN_CHUNK):
        cols = slice(c * FFN_CHUNK, (c + 1) * FFN_CHUNK)
        hid = jnp.maximum(_dot(h, w1_ref[:, cols]), 0.0)
        acc = acc + _dot((hid * hid).astype(BF16), w2_ref[cols, :])
    o_ref[...] = x1 + mod[:, 5 * D:6 * D] * acc


def _out_ffn(xs, a, mods, layer, j, wo, g, w1, w2, n_tiles):
    x_specs, x_args = _row_specs(xs)
    a_specs, a_args = _row_specs(a)
    row = lambda t: (t, 0)
    ka = a_args[0].shape[1]
    hbm = pl.BlockSpec(memory_space=pl.ANY)
    return pl.pallas_call(
        functools.partial(_out_ffn_kernel, n_x=len(x_args), n_a=len(a_args), layer=layer, j=j),
        grid=(n_tiles,),
        in_specs=x_specs + a_specs + [_mod_spec(layer), _layer_spec((1, D), layer), hbm, hbm, hbm],
        out_specs=pl.BlockSpec((TM, D), row),
        out_shape=jax.ShapeDtypeStruct((n_tiles * TM, D), F32),
        scratch_shapes=[pltpu.VMEM((ka, D), BF16), pltpu.VMEM((D, FFN_HIDDEN), BF16),
                        pltpu.VMEM((FFN_HIDDEN, D), BF16)] + _fill_scratch(),
        compiler_params=_params(1),
        name="out_ffn",
    )(*x_args, *a_args, mods, g, wo, w1, w2)


def _rot_perm_sign():
    q = QK_ROPE // 4
    perm = np.concatenate([np.arange(q, 2 * q), np.arange(0, q), np.arange(3 * q, 4 * q), np.arange(2 * q, 3 * q)])
    sign = np.concatenate([-np.ones(q), np.ones(q), -np.ones(q), np.ones(q)]).astype(np.float32)
    return perm, sign


def _gain_rows(gain):
    perm, _ = _rot_perm_sign()
    n = gain.shape[0]
    pad = jnp.zeros((n, LANES - QK_ROPE), F32)
    rows = jnp.stack([gain[:, :QK_NOPE],
                      jnp.concatenate([gain[:, QK_NOPE:], pad], axis=1),
                      jnp.concatenate([gain[:, QK_NOPE:][:, perm], pad], axis=1)], axis=1)
    return jnp.concatenate([rows, jnp.zeros((n, 8 - 3, LANES), F32)], axis=1)


def _mla_params(wq_a, q_a_norm, wq_b, wkv_a, kv_a_norm, wkv_b, q_norm, k_norm):
    perm, sign = _rot_perm_sign()
    n = wq_a.shape[0]
    def by_head_pair(first, second):
        r = first.shape[1]
        parts = [t.reshape(n, r, H // 2, 2, LANES) for t in (first, second)]
        return jnp.stack(parts, axis=3).reshape(n, r, 2 * H * LANES)

    wq_b = wq_b.reshape(n, Q_LORA, H, QK_DIM)
    q_rope = wq_b[..., QK_NOPE:]
    wqb = by_head_pair(wq_b[..., :QK_NOPE], jnp.concatenate([q_rope, q_rope[..., perm] * sign], axis=-1))
    k_rope = wkv_a[..., KV_LORA:]
    wkva = jnp.concatenate([wkv_a, k_rope[..., perm] * sign], axis=-1)
    wkv_b = wkv_b.reshape(n, KV_LORA, H, QK_NOPE + V_DIM)
    wkvb = by_head_pair(wkv_b[..., :QK_NOPE], wkv_b[..., QK_NOPE:])
    return dict(wqa=wq_a.astype(BF16), qan=q_a_norm.reshape(n, 1, Q_LORA), wqb=wqb.astype(BF16),
                wkva=wkva.astype(BF16), kvan=kv_a_norm.reshape(n, 1, KV_LORA), wkvb=wkvb.astype(BF16),
                gq=_gain_rows(q_norm), gk=_gain_rows(k_norm))


def _rope_tables():
    row = np.repeat(np.arange(S // GRID_W, dtype=np.float32), GRID_W)
    col = np.tile(np.arange(GRID_W, dtype=np.float32), S // GRID_W)
    half = QK_ROPE // 2
    inv = (np.float32(ROPE_THETA) ** (-np.arange(0, half, 2, dtype=np.float32) / np.float32(half))).astype(np.float32)
    ang_r = row[:, None] * inv[None, :]
    ang_c = col[:, None] * inv[None, :]
    ang = np.concatenate([ang_r, ang_r, ang_c, ang_c], axis=-1)
    cos = np.zeros((S + TM, LANES), np.float32)
    sin = np.zeros((S + TM, LANES), np.float32)
    cos[:S, :QK_ROPE] = np.cos(ang)
    sin[:S, :QK_ROPE] = np.sin(ang)
    cos[S:, :QK_ROPE] = 1.0
    return jnp.asarray(cos), jnp.asarray(sin)


def kernel(x, c, ctx, c_ctx, ada_w, ada_b, norm_mix_g, norm_ffn_g, mla_wq_a, mla_q_a_norm, mla_wq_b,
           mla_wkv_a, mla_kv_a_norm, mla_wkv_b, mla_q_norm, mla_k_norm, mla_wo, gm_w_in, gm_ln_g,
           gm_ln_b, gm_ws, gm_bs, gm_w_out, ffn_w1, ffn_w2):
    c8 = jnp.concatenate([c, c_ctx[None], jnp.zeros((MOD_ROWS - B - 1, D), F32)])
    mods = _adaln(c8, ada_w, ada_b).reshape(DEPTH * MOD_ROWS, 1, N_MOD * D)
    cos_t, sin_t = _rope_tables()

    g_mix = norm_mix_g.reshape(DEPTH, 1, D)
    g_ffn = norm_ffn_g.reshape(DEPTH, 1, D)
    w1, w2 = ffn_w1, ffn_w2
    mla = _mla_params(mla_wq_a, mla_q_a_norm, mla_wq_b, mla_wkv_a, mla_kv_a_norm, mla_wkv_b,
                      mla_q_norm, mla_k_norm)
    gm = dict(w_in=gm_w_in, ln_g=gm_ln_g.reshape(-1, 1, GM_HALF),
              ln_b=gm_ln_b.reshape(-1, 1, GM_HALF), ws=(0.5 * gm_ws).astype(BF16),
              bs=(0.5 * gm_bs).reshape(-1, GM_GROUPS, GM_CHUNK, 1))

    xs = (x.reshape(N_LAT, D), ctx.reshape(N_CTX, D))
    for i in range(DEPTH):
        j = i // 2
        if i % 2 == 0:
            q, k, v = _mla_proj(xs, mods, i, j, g_mix, mla, cos_t, sin_t)
            a = _attn_lat(_plain_softmax_ok(mla_q_norm[j], mla_k_norm[j]), q, k, v)
            ctx_live = i + 2 < DEPTH
            if ctx_live:
                a = (a, _attn_ctx(q, k, v))
            n_tiles = ALL_TILES if ctx_live else LAT_TILES
            xs = _out_ffn(xs, a, mods, i, j, mla_wo, g_ffn, w1, w2, n_tiles)
        else:
            n_tiles = LAT_TILES if i == DEPTH - 1 else ALL_TILES
            a = _gmlp(xs, mods, i, j, g_mix, gm, n_tiles)
            xs = _out_ffn(xs, a, mods, i, j, gm_w_out, g_ffn, w1, w2, n_tiles)
    return xs[:N_LAT].reshape(B, S, D)
```

```python
import functools

import jax
import jax.numpy as jnp
import numpy as np
from jax import lax
from jax.experimental import pallas as pl
from jax.experimental.pallas import tpu as pltpu

F32 = jnp.float32
BF16 = jnp.bfloat16

D = 1024
B = 4
S = 4096
DEPTH = 4
GRID_W = 64
CTX = 256
H = 8
QK_NOPE = 128
QK_ROPE = 64
V_DIM = 128
QK_DIM = QK_NOPE + QK_ROPE
Q_LORA = 384
KV_LORA = 256
ROPE_THETA = 10000.0
GM_CHUNK = 128
GM_GROUPS = 8
GM_HALF = 2 * D
GM_GROUP_DIM = GM_HALF // GM_GROUPS
FFN_HIDDEN = 4 * D
N_MOD = 6
EPS = 1e-6

N_LAT = B * S
N_CTX = B * CTX
N_ALL = N_LAT + N_CTX
LANES = 128
HEAD_PAD = 2 * LANES
TM = 512
TQ = 1024
TK = 512
ATTN_HEADS = 2
PLAIN_SCORE_LIMIT = 64.0
MOD_ROWS = 8
CTX_MOD_ROW = B
LAT_TILES = N_LAT // TM
ALL_TILES = N_ALL // TM
TILES_PER_BATCH = S // TM
ADA_TN = 1536
FFN_CHUNK = 1024
GM_VCHUNK = 512
FILL_BLOCK = (512, 1024)
VMEM_LIMIT = 56 * 1024 * 1024


def _params(grid_rank):
    return pltpu.CompilerParams(dimension_semantics=("arbitrary",) * grid_rank,
                                vmem_limit_bytes=VMEM_LIMIT)


def _mod_spec(layer):
    def index(t):
        return (layer * MOD_ROWS + jnp.where(t < LAT_TILES, t // TILES_PER_BATCH, CTX_MOD_ROW), 0, 0)
    return pl.BlockSpec((1, 1, N_MOD * D), index)


def _const_spec(shape):
    nd = len(shape)
    return pl.BlockSpec(shape, lambda *_: (0,) * nd, pipeline_mode=pl.Buffered(1))


def _layer_spec(shape, layer):
    nd = len(shape)
    return pl.BlockSpec((1,) + tuple(shape), lambda *_: (layer,) + (0,) * nd,
                        pipeline_mode=pl.Buffered(1))


def _row_specs(xs):
    if isinstance(xs, tuple):
        width = xs[0].shape[1]
        return [pl.BlockSpec((TM, width), lambda t: (jnp.minimum(t, LAT_TILES - 1), 0)),
                pl.BlockSpec((TM, width), lambda t: (jnp.maximum(t - LAT_TILES, 0), 0))], list(xs)
    return [pl.BlockSpec((TM, xs.shape[1]), lambda t: (t, 0))], [xs]


def _load_rows(x_refs):
    if len(x_refs) == 1:
        return x_refs[0][...]
    return jnp.where(pl.program_id(0) < LAT_TILES, x_refs[0][...], x_refs[1][...])


def _weight_blocks(w_hbm, layer, w_vmem):
    rows, cols = w_vmem.shape
    br, bc = FILL_BLOCK
    return [(w_hbm.at[layer, pl.ds(r, br), pl.ds(c, bc)], w_vmem.at[pl.ds(r, br), pl.ds(c, bc)])
            for r in range(0, rows, br) for c in range(0, cols, bc)]


def _fill_bf16_weights(blocks, stage_ref, sem_ref):
    @pl.when(pl.program_id(0) == 0)
    def _():
        def copy(i):
            return pltpu.make_async_copy(blocks[i][0], stage_ref.at[i % 2], sem_ref.at[i % 2])

        copy(0).start()
        for i in range(len(blocks)):
            if i + 1 < len(blocks):
                copy(i + 1).start()
            copy(i).wait()
            blocks[i][1][...] = stage_ref[i % 2].astype(BF16)


def _fill_scratch():
    return [pltpu.VMEM((2,) + FILL_BLOCK, F32), pltpu.SemaphoreType.DMA((2,))]


def _rms(x):
    return x * lax.rsqrt(jnp.mean(x * x, axis=-1, keepdims=True) + EPS)


def _modulate(x, g, shift, scale):
    return _rms(x) * (g * (1.0 + scale)) + shift


def _dot(a, b):
    return jnp.dot(a, b, preferred_element_type=F32)


def _adaln_kernel(c_ref, w_ref, b_ref, o_ref):
    c = c_ref[...]
    s = c * jax.nn.sigmoid(c)
    o_ref[0] = _dot(s.astype(BF16), w_ref[0].astype(BF16)) + b_ref[0]


def _adaln(c8, ada_w, ada_b):
    return pl.pallas_call(
        _adaln_kernel,
        grid=(DEPTH, N_MOD * D // ADA_TN),
        in_specs=[
            pl.BlockSpec((MOD_ROWS, D), lambda i, j: (0, 0)),
            pl.BlockSpec((1, D, ADA_TN), lambda i, j: (i, 0, j)),
            pl.BlockSpec((1, 1, ADA_TN), lambda i, j: (i, 0, j)),
        ],
        out_specs=pl.BlockSpec((1, MOD_ROWS, ADA_TN), lambda i, j: (i, 0, j)),
        out_shape=jax.ShapeDtypeStruct((DEPTH, MOD_ROWS, N_MOD * D), F32),
        compiler_params=_params(2),
        name="adaln",
    )(c8, ada_w, ada_b.reshape(DEPTH, 1, N_MOD * D))


def _mla_proj_kernel(*refs, n_x):
    x_refs = refs[:n_x]
    (mod_ref, g_ref, wqa_ref, qan_ref, wqb_ref, wkva_ref, kvan_ref, wkvb_ref, gq_ref, gk_ref,
     cos_ref, sin_ref, q_ref, k_ref, v_ref) = refs[n_x:]
    mod = mod_ref[0]
    x = _load_rows(x_refs)
    gq = gq_ref[0] * (QK_DIM ** -0.5 * np.log2(np.e))
    gk = gk_ref[0]
    inv_dim = 1.0 / QK_DIM
    lane_head = lax.broadcasted_iota(jnp.int32, (HEAD_PAD, HEAD_PAD), 0) // LANES
    pair_ones = (lane_head == lax.broadcasted_iota(jnp.int32, (HEAD_PAD, HEAD_PAD), 1) // LANES).astype(BF16)

    h = _modulate(x, g_ref[0], mod[:, 0:D], mod[:, D:2 * D]).astype(BF16)
    kva = _dot(h, wkva_ref[0])
    ckv = (_rms(kva[:, 0:KV_LORA]) * kvan_ref[0]).astype(BF16)
    cq = (_rms(_dot(h, wqa_ref[0])) * qan_ref[0]).astype(BF16)
    cos = cos_ref[...]
    sin = sin_ref[...]
    q_tabs = (cos * gq[1:2], sin * gq[2:3])
    k_tabs = (cos * gk[1:2], sin * gk[2:3])

    def rope(pair, tabs):
        return pair * tabs[0] + pltpu.roll(pair, QK_ROPE, axis=1) * tabs[1]

    kpair = kva[:, KV_LORA:KV_LORA + LANES]
    k_rope = rope(kpair, k_tabs)
    k_pair_sq = 0.5 * kpair * kpair
    pair_cols = 4 * LANES
    for hd in range(0, H, 2):
        kv4 = _dot(ckv, wkvb_ref[0, :, hd // 2 * pair_cols:(hd // 2 + 1) * pair_cols])
        kn = [kv4[:, i * LANES:(i + 1) * LANES] for i in range(2)]
        sq = jnp.concatenate([n * n + k_pair_sq for n in kn], axis=1)
        r = lax.rsqrt(_dot(sq.astype(BF16), pair_ones) * inv_dim + EPS)
        for i in range(2):
            ri = r[:, i * LANES:(i + 1) * LANES]
            k_ref[hd + i, :, 0:LANES] = (kn[i] * ri * gk[0:1]).astype(BF16)
            k_ref[hd + i, :, LANES:HEAD_PAD] = (k_rope * ri).astype(BF16)
            v_ref[hd + i] = kv4[:, (2 + i) * LANES:(3 + i) * LANES].astype(BF16)
    for hd in range(0, H, 2):
        q4 = _dot(cq, wqb_ref[0, :, hd // 2 * pair_cols:(hd // 2 + 1) * pair_cols])
        qn = [q4[:, i * LANES:(i + 1) * LANES] for i in range(2)]
        qpair = [q4[:, (2 + i) * LANES:(3 + i) * LANES] for i in range(2)]
        sq = jnp.concatenate([n * n + 0.5 * pr * pr for n, pr in zip(qn, qpair)], axis=1)
        r = lax.rsqrt(_dot(sq.astype(BF16), pair_ones) * inv_dim + EPS)
        for i in range(2):
            ri = r[:, i * LANES:(i + 1) * LANES]
            q_ref[hd + i, :, 0:LANES] = (qn[i] * ri * gq[0:1]).astype(BF16)
            q_ref[hd + i, :, LANES:HEAD_PAD] = (rope(qpair[i], q_tabs) * ri).astype(BF16)


def _mla_proj(xs, mods, layer, j, g, p, cos_t, sin_t):
    x_specs, x_args = _row_specs(xs)
    pos = lambda t: (jnp.where(t < LAT_TILES, t % TILES_PER_BATCH, TILES_PER_BATCH), 0)
    return pl.pallas_call(
        functools.partial(_mla_proj_kernel, n_x=len(x_args)),
        grid=(ALL_TILES,),
        in_specs=x_specs + [
            _mod_spec(layer),
            _layer_spec((1, D), layer),
            _layer_spec((D, Q_LORA), j),
            _layer_spec((1, Q_LORA), j),
            _layer_spec((Q_LORA, 2 * H * LANES), j),
            _layer_spec((D, KV_LORA + LANES), j),
            _layer_spec((1, KV_LORA), j),
            _layer_spec((KV_LORA, 2 * H * LANES), j),
            _layer_spec((8, LANES), j),
            _layer_spec((8, LANES), j),
            pl.BlockSpec((TM, LANES), pos),
            pl.BlockSpec((TM, LANES), pos),
        ],
        out_specs=[
            pl.BlockSpec((H, TM, HEAD_PAD), lambda t: (0, t, 0)),
            pl.BlockSpec((H, TM, HEAD_PAD), lambda t: (0, t, 0)),
            pl.BlockSpec((H, TM, V_DIM), lambda t: (0, t, 0)),
        ],
        out_shape=[
            jax.ShapeDtypeStruct((H, N_ALL, HEAD_PAD), BF16),
            jax.ShapeDtypeStruct((H, N_ALL, HEAD_PAD), BF16),
            jax.ShapeDtypeStruct((H, N_ALL, V_DIM), BF16),
        ],
        compiler_params=_params(1),
        name="mla_proj",
    )(*x_args, mods, g, p["wqa"], p["qan"], p["wqb"], p["wkva"], p["kvan"], p["wkvb"],
      p["gq"], p["gk"], cos_t, sin_t)


def _ones_column(rows):
    return (lax.broadcasted_iota(jnp.int32, (rows, LANES), 1) == 0).astype(BF16)


def _scores(q, k):
    return lax.dot_general(q, k, (((1,), (1,)), ((), ())), preferred_element_type=F32)


def _softmax_step(s, v, m, acc):
    m_new = jnp.max(s, axis=-1, keepdims=True)
    if m is not None:
        m_new = jnp.maximum(m, m_new)
    p = jnp.exp2(s - m_new).astype(BF16)
    pv = _dot(p, jnp.concatenate([v, _ones_column(v.shape[0])], axis=1))
    if m is None:
        return m_new, pv
    return m_new, jnp.exp2(m - m_new) * acc + pv


def _plain_step(s, v, m, acc):
    del m
    pv = _dot(jnp.exp2(s).astype(BF16), jnp.concatenate([v, _ones_column(v.shape[0])], axis=1))
    return None, pv if acc is None else acc + pv


def _attn_heads(step, q_ref, kl_ref, kc_ref, vl_ref, vc_ref, o_ref):
    state = [(None, None)] * ATTN_HEADS
    for c in range(S // TK):
        rows = slice(c * TK, (c + 1) * TK)
        for i in range(ATTN_HEADS):
            state[i] = step(_scores(q_ref[i], kl_ref[i, rows, :]), vl_ref[i, rows, :], *state[i])
    for i in range(ATTN_HEADS):
        _, acc = step(_scores(q_ref[i], kc_ref[i]), vc_ref[i], *state[i])
        o_ref[:, i * V_DIM:(i + 1) * V_DIM] = (acc[:, 0:V_DIM] / acc[:, V_DIM:V_DIM + 1]).astype(BF16)


def _attn_lat_kernel(plain_ref, *refs):
    @pl.when(plain_ref[0] != 0)
    def _():
        _attn_heads(_plain_step, *refs)

    @pl.when(plain_ref[0] == 0)
    def _():
        _attn_heads(_softmax_step, *refs)


def _plain_softmax_ok(q_gain, k_gain):
    bound = QK_DIM * jnp.max(jnp.abs(q_gain)) * jnp.max(jnp.abs(k_gain)) * (QK_DIM ** -0.5 * np.log2(np.e)) * 1.02
    return (bound <= PLAIN_SCORE_LIMIT).astype(jnp.int32).reshape(1)


def _attn_lat(plain_ok, q, k, v):
    ctx_blk = N_LAT // CTX
    return pl.pallas_call(
        _attn_lat_kernel,
        grid=(B, H // ATTN_HEADS, S // TQ),
        in_specs=[
            pl.BlockSpec(memory_space=pltpu.SMEM),
            pl.BlockSpec((ATTN_HEADS, TQ, HEAD_PAD), lambda b, h, i: (h, b * (S // TQ) + i, 0)),
            pl.BlockSpec((ATTN_HEADS, S, HEAD_PAD), lambda b, h, i: (h, b, 0)),
            pl.BlockSpec((ATTN_HEADS, CTX, HEAD_PAD), lambda b, h, i: (h, ctx_blk + b, 0)),
            pl.BlockSpec((ATTN_HEADS, S, V_DIM), lambda b, h, i: (h, b, 0)),
            pl.BlockSpec((ATTN_HEADS, CTX, V_DIM), lambda b, h, i: (h, ctx_blk + b, 0)),
        ],
        out_specs=pl.BlockSpec((TQ, ATTN_HEADS * V_DIM), lambda b, h, i: (b * (S // TQ) + i, h)),
        out_shape=jax.ShapeDtypeStruct((N_LAT, H * V_DIM), BF16),
        compiler_params=_params(3),
        name="attn_lat",
    )(plain_ok, q, k, k, v, v)


def _attn_ctx_kernel(q_ref, k_ref, v_ref, o_ref):
    for hd in range(H):
        _, acc = _softmax_step(_scores(q_ref[hd], k_ref[hd]), v_ref[hd], None, None)
        o_ref[:, hd * V_DIM:(hd + 1) * V_DIM] = (acc[:, 0:V_DIM] / acc[:, V_DIM:V_DIM + 1]).astype(BF16)


def _attn_ctx(q, k, v):
    blk = lambda b: (0, N_LAT // CTX + b, 0)
    return pl.pallas_call(
        _attn_ctx_kernel,
        grid=(B,),
        in_specs=[
            pl.BlockSpec((H, CTX, HEAD_PAD), blk),
            pl.BlockSpec((H, CTX, HEAD_PAD), blk),
            pl.BlockSpec((H, CTX, V_DIM), blk),
        ],
        out_specs=pl.BlockSpec((CTX, H * V_DIM), lambda b: (b, 0)),
        out_shape=jax.ShapeDtypeStruct((N_CTX, H * V_DIM), BF16),
        compiler_params=_params(1),
        name="attn_ctx",
    )(q, k, v)


def _gmlp_kernel(x_ref, mod_ref, g_ref, win_hbm, lng_ref, lnb_ref, ws_ref, bs_ref, a_ref,
                 win_ref, stage_ref, sem_ref, *, j):
    _fill_bf16_weights(_weight_blocks(win_hbm, j, win_ref), stage_ref, sem_ref)
    mod = mod_ref[0]
    h = _modulate(x_ref[...], g_ref[0], mod[:, 0:D], mod[:, D:2 * D]).astype(BF16)
    sqrt_half = np.float32(np.sqrt(0.5))

    def gelu2(z):
        return z * (1.0 + lax.erf(z * sqrt_half))

    def lane_partial(a):
        acc = a[:, 0:LANES]
        for i in range(1, a.shape[1] // LANES):
            acc = acc + a[:, i * LANES:(i + 1) * LANES]
        return acc

    zks = []
    for c in range(GM_HALF // GM_VCHUNK):
        z = gelu2(_dot(h, win_ref[:, GM_HALF + c * GM_VCHUNK:GM_HALF + (c + 1) * GM_VCHUNK]))
        if c == 0:
            k = jnp.mean(z, axis=-1, keepdims=True)
        zk = z - k
        zks.append(zk)
        s1 = lane_partial(zk) if c == 0 else s1 + lane_partial(zk)
        s2 = lane_partial(zk * zk) if c == 0 else s2 + lane_partial(zk * zk)
    d = jnp.sum(s1, axis=-1, keepdims=True) * (1.0 / GM_HALF)
    var = jnp.sum(s2, axis=-1, keepdims=True) * (1.0 / GM_HALF) - d * d
    r = lax.rsqrt(var + 4.0 * EPS)
    vn = [((zk - d) * r).astype(BF16) for zk in zks]

    per_chunk = GM_VCHUNK // GM_GROUP_DIM
    for gi in range(GM_GROUPS):
        cols = slice(gi * GM_GROUP_DIM, (gi + 1) * GM_GROUP_DIM)
        u = gelu2(_dot(h, win_ref[:, cols]))
        ws = ws_ref[0, gi]
        bias = lnb_ref[0][:, cols] * jnp.sum(ws.astype(F32), axis=1, keepdims=True) + bs_ref[0, gi]
        gain = lng_ref[0][:, cols]
        vcols = slice((gi % per_chunk) * GM_GROUP_DIM, (gi % per_chunk + 1) * GM_GROUP_DIM)
        for c in range(TM // GM_CHUNK):
            rows = slice(c * GM_CHUNK, (c + 1) * GM_CHUNK)
            mixed = _dot(ws, vn[gi // per_chunk][rows, vcols]) * gain + bias
            a_ref[rows, cols] = (u[rows] * mixed).astype(BF16)


def _gmlp(xs, mods, layer, j, g, p, n_tiles):
    row = lambda t: (t, 0)
    return pl.pallas_call(
        functools.partial(_gmlp_kernel, j=j),
        grid=(n_tiles,),
        in_specs=[
            pl.BlockSpec((TM, D), row),
            _mod_spec(layer),
            _layer_spec((1, D), layer),
            pl.BlockSpec(memory_space=pl.ANY),
            _layer_spec((1, GM_HALF), j),
            _layer_spec((1, GM_HALF), j),
            _layer_spec((GM_GROUPS, GM_CHUNK, GM_CHUNK), j),
            _layer_spec((GM_GROUPS, GM_CHUNK, 1), j),
        ],
        out_specs=pl.BlockSpec((TM, GM_HALF), row),
        out_shape=jax.ShapeDtypeStruct((n_tiles * TM, GM_HALF), BF16),
        scratch_shapes=[pltpu.VMEM((D, 2 * GM_HALF), BF16)] + _fill_scratch(),
        compiler_params=_params(1),
        name="gmlp_mix",
    )(xs, mods, g, p["w_in"], p["ln_g"], p["ln_b"], p["ws"], p["bs"])


def _out_ffn_kernel(*refs, n_x, n_a, layer, j):
    x_refs, a_refs = refs[:n_x], refs[n_x:n_x + n_a]
    mod_ref, g_ref, wo_hbm, w1_hbm, w2_hbm, o_ref, wo_ref, w1_ref, w2_ref, stage_ref, sem_ref = refs[n_x + n_a:]
    _fill_bf16_weights(_weight_blocks(wo_hbm, j, wo_ref) + _weight_blocks(w1_hbm, layer, w1_ref)
                       + _weight_blocks(w2_hbm, layer, w2_ref), stage_ref, sem_ref)
    mod = mod_ref[0]
    x1 = _load_rows(x_refs) + mod[:, 2 * D:3 * D] * _dot(_load_rows(a_refs), wo_ref[...])
    h = _modulate(x1, g_ref[0], mod[:, 3 * D:4 * D], mod[:, 4 * D:5 * D]).astype(BF16)
    acc = jnp.zeros((TM, D), F32)
    for c in range(FFN_HIDDEN // FFN_CHUNK):
        cols = slice(c * FFN_CHUNK, (c + 1) * FFN_CHUNK)
        hid = jnp.maximum(_dot(h, w1_ref[:, cols]), 0.0)
        acc = acc + _dot((hid * hid).astype(BF16), w2_ref[cols, :])
    o_ref[...] = x1 + mod[:, 5 * D:6 * D] * acc


def _out_ffn(xs, a, mods, layer, j, wo, g, w1, w2, n_tiles):
    x_specs, x_args = _row_specs(xs)
    a_specs, a_args = _row_specs(a)
    row = lambda t: (t, 0)
    ka = a_args[0].shape[1]
    hbm = pl.BlockSpec(memory_space=pl.ANY)
    return pl.pallas_call(
        functools.partial(_out_ffn_kernel, n_x=len(x_args), n_a=len(a_args), layer=layer, j=j),
        grid=(n_tiles,),
        in_specs=x_specs + a_specs + [_mod_spec(layer), _layer_spec((1, D), layer), hbm, hbm, hbm],
        out_specs=pl.BlockSpec((TM, D), row),
        out_shape=jax.ShapeDtypeStruct((n_tiles * TM, D), F32),
        scratch_shapes=[pltpu.VMEM((ka, D), BF16), pltpu.VMEM((D, FFN_HIDDEN), BF16),
                        pltpu.VMEM((FFN_HIDDEN, D), BF16)] + _fill_scratch(),
        compiler_params=_params(1),
        name="out_ffn",
    )(*x_args, *a_args, mods, g, wo, w1, w2)


def _rot_perm_sign():
    q = QK_ROPE // 4
    perm = np.concatenate([np.arange(q, 2 * q), np.arange(0, q), np.arange(3 * q, 4 * q), np.arange(2 * q, 3 * q)])
    sign = np.concatenate([-np.ones(q), np.ones(q), -np.ones(q), np.ones(q)]).astype(np.float32)
    return perm, sign


def _gain_rows(gain):
    perm, _ = _rot_perm_sign()
    n = gain.shape[0]
    pad = jnp.zeros((n, LANES - QK_ROPE), F32)
    rows = jnp.stack([gain[:, :QK_NOPE],
                      jnp.concatenate([gain[:, QK_NOPE:], pad], axis=1),
                      jnp.concatenate([gain[:, QK_NOPE:][:, perm], pad], axis=1)], axis=1)
    return jnp.concatenate([rows, jnp.zeros((n, 8 - 3, LANES), F32)], axis=1)


def _mla_params(wq_a, q_a_norm, wq_b, wkv_a, kv_a_norm, wkv_b, q_norm, k_norm):
    perm, sign = _rot_perm_sign()
    n = wq_a.shape[0]
    def by_head_pair(first, second):
        r = first.shape[1]
        parts = [t.reshape(n, r, H // 2, 2, LANES) for t in (first, second)]
        return jnp.stack(parts, axis=3).reshape(n, r, 2 * H * LANES)

    wq_b = wq_b.reshape(n, Q_LORA, H, QK_DIM)
    q_rope = wq_b[..., QK_NOPE:]
    wqb = by_head_pair(wq_b[..., :QK_NOPE], jnp.concatenate([q_rope, q_rope[..., perm] * sign], axis=-1))
    k_rope = wkv_a[..., KV_LORA:]
    wkva = jnp.concatenate([wkv_a, k_rope[..., perm] * sign], axis=-1)
    wkv_b = wkv_b.reshape(n, KV_LORA, H, QK_NOPE + V_DIM)
    wkvb = by_head_pair(wkv_b[..., :QK_NOPE], wkv_b[..., QK_NOPE:])
    return dict(wqa=wq_a.astype(BF16), qan=q_a_norm.reshape(n, 1, Q_LORA), wqb=wqb.astype(BF16),
                wkva=wkva.astype(BF16), kvan=kv_a_norm.reshape(n, 1, KV_LORA), wkvb=wkvb.astype(BF16),
                gq=_gain_rows(q_norm), gk=_gain_rows(k_norm))


def _rope_tables():
    row = np.repeat(np.arange(S // GRID_W, dtype=np.float32), GRID_W)
    col = np.tile(np.arange(GRID_W, dtype=np.float32), S // GRID_W)
    half = QK_ROPE // 2
    inv = (np.float32(ROPE_THETA) ** (-np.arange(0, half, 2, dtype=np.float32) / np.float32(half))).astype(np.float32)
    ang_r = row[:, None] * inv[None, :]
    ang_c = col[:, None] * inv[None, :]
    ang = np.concatenate([ang_r, ang_r, ang_c, ang_c], axis=-1)
    cos = np.zeros((S + TM, LANES), np.float32)
    sin = np.zeros((S + TM, LANES), np.float32)
    cos[:S, :QK_ROPE] = np.cos(ang)
    sin[:S, :QK_ROPE] = np.sin(ang)
    cos[S:, :QK_ROPE] = 1.0
    return jnp.asarray(cos), jnp.asarray(sin)


def kernel(x, c, ctx, c_ctx, ada_w, ada_b, norm_mix_g, norm_ffn_g, mla_wq_a, mla_q_a_norm, mla_wq_b,
           mla_wkv_a, mla_kv_a_norm, mla_wkv_b, mla_q_norm, mla_k_norm, mla_wo, gm_w_in, gm_ln_g,
           gm_ln_b, gm_ws, gm_bs, gm_w_out, ffn_w1, ffn_w2):
    c8 = jnp.concatenate([c, c_ctx[None], jnp.zeros((MOD_ROWS - B - 1, D), F32)])
    mods = _adaln(c8, ada_w, ada_b).reshape(DEPTH * MOD_ROWS, 1, N_MOD * D)
    cos_t, sin_t = _rope_tables()

    g_mix = norm_mix_g.reshape(DEPTH, 1, D)
    g_ffn = norm_ffn_g.reshape(DEPTH, 1, D)
    w1, w2 = ffn_w1, ffn_w2
    mla = _mla_params(mla_wq_a, mla_q_a_norm, mla_wq_b, mla_wkv_a, mla_kv_a_norm, mla_wkv_b,
                      mla_q_norm, mla_k_norm)
    gm = dict(w_in=gm_w_in, ln_g=gm_ln_g.reshape(-1, 1, GM_HALF),
              ln_b=gm_ln_b.reshape(-1, 1, GM_HALF), ws=(0.5 * gm_ws).astype(BF16),
              bs=(0.5 * gm_bs).reshape(-1, GM_GROUPS, GM_CHUNK, 1))

    xs = (x.reshape(N_LAT, D), ctx.reshape(N_CTX, D))
    for i in range(DEPTH):
        j = i // 2
        if i % 2 == 0:
            q, k, v = _mla_proj(xs, mods, i, j, g_mix, mla, cos_t, sin_t)
            a = _attn_lat(_plain_softmax_ok(mla_q_norm[j], mla_k_norm[j]), q, k, v)
            ctx_live = i + 2 < DEPTH
            if ctx_live:
                a = (a, _attn_ctx(q, k, v))
            n_tiles = ALL_TILES if ctx_live else LAT_TILES
            xs = _out_ffn(xs, a, mods, i, j, mla_wo, g_ffn, w1, w2, n_tiles)
        else:
            n_tiles = LAT_TILES if i == DEPTH - 1 else ALL_TILES
            a = _gmlp(xs, mods, i, j, g_mix, gm, n_tiles)
            xs = _out_ffn(xs, a, mods, i, j, gm_w_out, g_ffn, w1, w2, n_tiles)
    return xs[:N_LAT].reshape(B, S, D)
```

```python
import functools

import jax
import jax.numpy as jnp
import numpy as np
from jax import lax
from jax.experimental import pallas as pl
from jax.experimental.pallas import tpu as pltpu

F32 = jnp.float32
BF16 = jnp.bfloat16

D = 1024
B = 4
S = 4096
DEPTH = 4
GRID_W = 64
CTX = 256
H = 8
QK_NOPE = 128
QK_ROPE = 64
V_DIM = 128
QK_DIM = QK_NOPE + QK_ROPE
Q_LORA = 384
KV_LORA = 256
ROPE_THETA = 10000.0
GM_CHUNK = 128
GM_GROUPS = 8
GM_HALF = 2 * D
GM_GROUP_DIM = GM_HALF // GM_GROUPS
FFN_HIDDEN = 4 * D
N_MOD = 6
EPS = 1e-6

N_LAT = B * S
N_CTX = B * CTX
N_ALL = N_LAT + N_CTX
LANES = 128
HEAD_PAD = 2 * LANES
TM = 512
TQ = 1024
TK = 512
ATTN_HEADS = 2
PLAIN_SCORE_LIMIT = 64.0
MOD_ROWS = 8
CTX_MOD_ROW = B
LAT_TILES = N_LAT // TM
ALL_TILES = N_ALL // TM
TILES_PER_BATCH = S // TM
ADA_TN = 1536
FFN_CHUNK = 1024
GM_VCHUNK = 512
FILL_BLOCK = (512, 1024)
assert FFN_CHUNK == FILL_BLOCK[1] and FFN_CHUNK % FILL_BLOCK[0] == 0
VMEM_LIMIT = 56 * 1024 * 1024


def _params(grid_rank):
    return pltpu.CompilerParams(dimension_semantics=("arbitrary",) * grid_rank,
                                vmem_limit_bytes=VMEM_LIMIT)


def _mod_spec(layer):
    def index(t):
        return (layer * MOD_ROWS + jnp.where(t < LAT_TILES, t // TILES_PER_BATCH, CTX_MOD_ROW), 0, 0)
    return pl.BlockSpec((1, 1, N_MOD * D), index)


def _const_spec(shape):
    nd = len(shape)
    return pl.BlockSpec(shape, lambda *_: (0,) * nd, pipeline_mode=pl.Buffered(1))


def _layer_spec(shape, layer):
    nd = len(shape)
    return pl.BlockSpec((1,) + tuple(shape), lambda *_: (layer,) + (0,) * nd,
                        pipeline_mode=pl.Buffered(1))


def _row_specs(xs):
    if isinstance(xs, tuple):
        width = xs[0].shape[1]
        return [pl.BlockSpec((TM, width), lambda t: (jnp.minimum(t, LAT_TILES - 1), 0)),
                pl.BlockSpec((TM, width), lambda t: (jnp.maximum(t - LAT_TILES, 0), 0))], list(xs)
    return [pl.BlockSpec((TM, xs.shape[1]), lambda t: (t, 0))], [xs]


def _load_rows(x_refs):
    if len(x_refs) == 1:
        return x_refs[0][...]
    return jnp.where(pl.program_id(0) < LAT_TILES, x_refs[0][...], x_refs[1][...])


def _weight_blocks(w_hbm, layer, w_vmem):
    rows, cols = w_vmem.shape
    br, bc = FILL_BLOCK
    return [(w_hbm.at[layer, pl.ds(r, br), pl.ds(c, bc)], w_vmem.at[pl.ds(r, br), pl.ds(c, bc)])
            for r in range(0, rows, br) for c in range(0, cols, bc)]


class _WeightFill:
    def __init__(self, blocks, stage_ref, sem_ref):
        self.blocks, self.stage, self.sem, self.done = blocks, stage_ref, sem_ref, 0
        self._copy(0).start()

    def _copy(self, i):
        return pltpu.make_async_copy(self.blocks[i][0], self.stage.at[i % 2], self.sem.at[i % 2])

    def ensure(self, count):
        while self.done < count:
            i = self.done
            if i + 1 < len(self.blocks):
                self._copy(i + 1).start()
            self._copy(i).wait()
            self.blocks[i][1][...] = self.stage[i % 2].astype(BF16)
            self.done += 1


def _fill_scratch():
    return [pltpu.VMEM((2,) + FILL_BLOCK, F32), pltpu.SemaphoreType.DMA((2,))]


def _rms(x):
    return x * lax.rsqrt(jnp.mean(x * x, axis=-1, keepdims=True) + EPS)


def _modulate(x, g, shift, scale):
    return _rms(x) * (g * (1.0 + scale)) + shift


def _dot(a, b):
    return jnp.dot(a, b, preferred_element_type=F32)


def _adaln_kernel(c_ref, w_ref, b_ref, o_ref):
    c = c_ref[...]
    s = c * jax.nn.sigmoid(c)
    o_ref[0] = _dot(s.astype(BF16), w_ref[0].astype(BF16)) + b_ref[0]


def _adaln(c8, ada_w, ada_b):
    return pl.pallas_call(
        _adaln_kernel,
        grid=(DEPTH, N_MOD * D // ADA_TN),
        in_specs=[
            pl.BlockSpec((MOD_ROWS, D), lambda i, j: (0, 0)),
            pl.BlockSpec((1, D, ADA_TN), lambda i, j: (i, 0, j)),
            pl.BlockSpec((1, 1, ADA_TN), lambda i, j: (i, 0, j)),
        ],
        out_specs=pl.BlockSpec((1, MOD_ROWS, ADA_TN), lambda i, j: (i, 0, j)),
        out_shape=jax.ShapeDtypeStruct((DEPTH, MOD_ROWS, N_MOD * D), F32),
        compiler_params=_params(2),
        name="adaln",
    )(c8, ada_w, ada_b.reshape(DEPTH, 1, N_MOD * D))


def _mla_proj_kernel(*refs, n_x):
    x_refs = refs[:n_x]
    (mod_ref, g_ref, wqa_ref, qan_ref, wqb_ref, wkva_ref, kvan_ref, wkvb_ref, gq_ref, gk_ref,
     cos_ref, sin_ref, q_ref, k_ref, v_ref) = refs[n_x:]
    mod = mod_ref[0]
    x = _load_rows(x_refs)
    gq = gq_ref[0] * (QK_DIM ** -0.5 * np.log2(np.e))
    gk = gk_ref[0]
    inv_dim = 1.0 / QK_DIM
    lane_head = lax.broadcasted_iota(jnp.int32, (HEAD_PAD, HEAD_PAD), 0) // LANES
    pair_ones = (lane_head == lax.broadcasted_iota(jnp.int32, (HEAD_PAD, HEAD_PAD), 1) // LANES).astype(BF16)

    h = _modulate(x, g_ref[0], mod[:, 0:D], mod[:, D:2 * D]).astype(BF16)
    kva = _dot(h, wkva_ref[0])
    ckv = (_rms(kva[:, 0:KV_LORA]) * kvan_ref[0]).astype(BF16)
    cq = (_rms(_dot(h, wqa_ref[0])) * qan_ref[0]).astype(BF16)
    cos = cos_ref[...]
    sin = sin_ref[...]
    q_tabs = (cos * gq[1:2], sin * gq[2:3])
    k_tabs = (cos * gk[1:2], sin * gk[2:3])

    def rope(pair, tabs):
        return pair * tabs[0] + pltpu.roll(pair, QK_ROPE, axis=1) * tabs[1]

    kpair = kva[:, KV_LORA:KV_LORA + LANES]
    k_rope = rope(kpair, k_tabs)
    k_pair_sq = 0.5 * kpair * kpair
    pair_cols = 4 * LANES
    for hd in range(0, H, 2):
        kv4 = _dot(ckv, wkvb_ref[0, :, hd // 2 * pair_cols:(hd // 2 + 1) * pair_cols])
        kn = [kv4[:, i * LANES:(i + 1) * LANES] for i in range(2)]
        sq = jnp.concatenate([n * n + k_pair_sq for n in kn], axis=1)
        r = lax.rsqrt(_dot(sq.astype(BF16), pair_ones) * inv_dim + EPS)
        for i in range(2):
            ri = r[:, i * LANES:(i + 1) * LANES]
            k_ref[hd + i, :, 0:LANES] = (kn[i] * ri * gk[0:1]).astype(BF16)
            k_ref[hd + i, :, LANES:HEAD_PAD] = (k_rope * ri).astype(BF16)
            v_ref[hd + i] = kv4[:, (2 + i) * LANES:(3 + i) * LANES].astype(BF16)
    for hd in range(0, H, 2):
        q4 = _dot(cq, wqb_ref[0, :, hd // 2 * pair_cols:(hd // 2 + 1) * pair_cols])
        qn = [q4[:, i * LANES:(i + 1) * LANES] for i in range(2)]
        qpair = [q4[:, (2 + i) * LANES:(3 + i) * LANES] for i in range(2)]
        sq = jnp.concatenate([n * n + 0.5 * pr * pr for n, pr in zip(qn, qpair)], axis=1)
        r = lax.rsqrt(_dot(sq.astype(BF16), pair_ones) * inv_dim + EPS)
        for i in range(2):
            ri = r[:, i * LANES:(i + 1) * LANES]
            q_ref[hd + i, :, 0:LANES] = (qn[i] * ri * gq[0:1]).astype(BF16)
            q_ref[hd + i, :, LANES:HEAD_PAD] = (rope(qpair[i], q_tabs) * ri).astype(BF16)


def _mla_proj(xs, mods, layer, j, g, p, cos_t, sin_t):
    x_specs, x_args = _row_specs(xs)
    pos = lambda t: (jnp.where(t < LAT_TILES, t % TILES_PER_BATCH, TILES_PER_BATCH), 0)
    return pl.pallas_call(
        functools.partial(_mla_proj_kernel, n_x=len(x_args)),
        grid=(ALL_TILES,),
        in_specs=x_specs + [
            _mod_spec(layer),
            _layer_spec((1, D), layer),
            _layer_spec((D, Q_LORA), j),
            _layer_spec((1, Q_LORA), j),
            _layer_spec((Q_LORA, 2 * H * LANES), j),
            _layer_spec((D, KV_LORA + LANES), j),
            _layer_spec((1, KV_LORA), j),
            _layer_spec((KV_LORA, 2 * H * LANES), j),
            _layer_spec((8, LANES), j),
            _layer_spec((8, LANES), j),
            pl.BlockSpec((TM, LANES), pos),
            pl.BlockSpec((TM, LANES), pos),
        ],
        out_specs=[
            pl.BlockSpec((H, TM, HEAD_PAD), lambda t: (0, t, 0)),
            pl.BlockSpec((H, TM, HEAD_PAD), lambda t: (0, t, 0)),
            pl.BlockSpec((H, TM, V_DIM), lambda t: (0, t, 0)),
        ],
        out_shape=[
            jax.ShapeDtypeStruct((H, N_ALL, HEAD_PAD), BF16),
            jax.ShapeDtypeStruct((H, N_ALL, HEAD_PAD), BF16),
            jax.ShapeDtypeStruct((H, N_ALL, V_DIM), BF16),
        ],
        compiler_params=_params(1),
        name="mla_proj",
    )(*x_args, mods, g, p["wqa"], p["qan"], p["wqb"], p["wkva"], p["kvan"], p["wkvb"],
      p["gq"], p["gk"], cos_t, sin_t)


def _ones_column(rows):
    return (lax.broadcasted_iota(jnp.int32, (rows, LANES), 1) == 0).astype(BF16)


def _scores(q, k):
    return lax.dot_general(q, k, (((1,), (1,)), ((), ())), preferred_element_type=F32)


def _softmax_step(s, v, m, acc):
    m_new = jnp.max(s, axis=-1, keepdims=True)
    if m is not None:
        m_new = jnp.maximum(m, m_new)
    p = jnp.exp2(s - m_new).astype(BF16)
    pv = _dot(p, jnp.concatenate([v, _ones_column(v.shape[0])], axis=1))
    if m is None:
        return m_new, pv
    return m_new, jnp.exp2(m - m_new) * acc + pv


def _plain_step(s, v, m, acc):
    del m
    pv = _dot(jnp.exp2(s).astype(BF16), jnp.concatenate([v, _ones_column(v.shape[0])], axis=1))
    return None, pv if acc is None else acc + pv


def _attn_heads(step, q_ref, kl_ref, kc_ref, vl_ref, vc_ref, o_ref):
    state = [(None, None)] * ATTN_HEADS
    for c in range(S // TK):
        rows = slice(c * TK, (c + 1) * TK)
        for i in range(ATTN_HEADS):
            state[i] = step(_scores(q_ref[i], kl_ref[i, rows, :]), vl_ref[i, rows, :], *state[i])
    for i in range(ATTN_HEADS):
        _, acc = step(_scores(q_ref[i], kc_ref[i]), vc_ref[i], *state[i])
        o_ref[:, i * V_DIM:(i + 1) * V_DIM] = (acc[:, 0:V_DIM] / acc[:, V_DIM:V_DIM + 1]).astype(BF16)


def _attn_lat_kernel(plain_ref, *refs):
    @pl.when(plain_ref[0] != 0)
    def _():
        _attn_heads(_plain_step, *refs)

    @pl.when(plain_ref[0] == 0)
    def _():
        _attn_heads(_softmax_step, *refs)


def _plain_softmax_ok(q_gain, k_gain):
    bound = QK_DIM * jnp.max(jnp.abs(q_gain)) * jnp.max(jnp.abs(k_gain)) * (QK_DIM ** -0.5 * np.log2(np.e)) * 1.02
    return (bound <= PLAIN_SCORE_LIMIT).astype(jnp.int32).reshape(1)


def _attn_lat(plain_ok, q, k, v):
    ctx_blk = N_LAT // CTX
    return pl.pallas_call(
        _attn_lat_kernel,
        grid=(B, H // ATTN_HEADS, S // TQ),
        in_specs=[
            pl.BlockSpec(memory_space=pltpu.SMEM),
            pl.BlockSpec((ATTN_HEADS, TQ, HEAD_PAD), lambda b, h, i: (h, b * (S // TQ) + i, 0)),
            pl.BlockSpec((ATTN_HEADS, S, HEAD_PAD), lambda b, h, i: (h, b, 0)),
            pl.BlockSpec((ATTN_HEADS, CTX, HEAD_PAD), lambda b, h, i: (h, ctx_blk + b, 0)),
            pl.BlockSpec((ATTN_HEADS, S, V_DIM), lambda b, h, i: (h, b, 0)),
            pl.BlockSpec((ATTN_HEADS, CTX, V_DIM), lambda b, h, i: (h, ctx_blk + b, 0)),
        ],
        out_specs=pl.BlockSpec((TQ, ATTN_HEADS * V_DIM), lambda b, h, i: (b * (S // TQ) + i, h)),
        out_shape=jax.ShapeDtypeStruct((N_LAT, H * V_DIM), BF16),
        compiler_params=_params(3),
        name="attn_lat",
    )(plain_ok, q, k, k, v, v)


def _attn_ctx_kernel(q_ref, k_ref, v_ref, o_ref):
    for hd in range(H):
        _, acc = _softmax_step(_scores(q_ref[hd], k_ref[hd]), v_ref[hd], None, None)
        o_ref[:, hd * V_DIM:(hd + 1) * V_DIM] = (acc[:, 0:V_DIM] / acc[:, V_DIM:V_DIM + 1]).astype(BF16)


def _attn_ctx(q, k, v):
    blk = lambda b: (0, N_LAT // CTX + b, 0)
    return pl.pallas_call(
        _attn_ctx_kernel,
        grid=(B,),
        in_specs=[
            pl.BlockSpec((H, CTX, HEAD_PAD), blk),
            pl.BlockSpec((H, CTX, HEAD_PAD), blk),
            pl.BlockSpec((H, CTX, V_DIM), blk),
        ],
        out_specs=pl.BlockSpec((CTX, H * V_DIM), lambda b: (b, 0)),
        out_shape=jax.ShapeDtypeStruct((N_CTX, H * V_DIM), BF16),
        compiler_params=_params(1),
        name="attn_ctx",
    )(q, k, v)


def _gmlp_kernel(x_ref, mod_ref, g_ref, win_hbm, lng_ref, lnb_ref, ws_ref, bs_ref, a_ref,
                 win_ref, stage_ref, sem_ref, *, j):
    @pl.when(pl.program_id(0) == 0)
    def _():
        blocks = _weight_blocks(win_hbm, j, win_ref)
        _WeightFill(blocks, stage_ref, sem_ref).ensure(len(blocks))

    mod = mod_ref[0]
    h = _modulate(x_ref[...], g_ref[0], mod[:, 0:D], mod[:, D:2 * D]).astype(BF16)
    sqrt_half = np.float32(np.sqrt(0.5))

    def gelu2(z):
        return z * (1.0 + lax.erf(z * sqrt_half))

    def lane_partial(a):
        acc = a[:, 0:LANES]
        for i in range(1, a.shape[1] // LANES):
            acc = acc + a[:, i * LANES:(i + 1) * LANES]
        return acc

    zks = []
    for c in range(GM_HALF // GM_VCHUNK):
        z = gelu2(_dot(h, win_ref[:, GM_HALF + c * GM_VCHUNK:GM_HALF + (c + 1) * GM_VCHUNK]))
        if c == 0:
            k = jnp.mean(z, axis=-1, keepdims=True)
        zk = z - k
        zks.append(zk)
        s1 = lane_partial(zk) if c == 0 else s1 + lane_partial(zk)
        s2 = lane_partial(zk * zk) if c == 0 else s2 + lane_partial(zk * zk)
    d = jnp.sum(s1, axis=-1, keepdims=True) * (1.0 / GM_HALF)
    var = jnp.sum(s2, axis=-1, keepdims=True) * (1.0 / GM_HALF) - d * d
    r = lax.rsqrt(var + 4.0 * EPS)
    vn = [((zk - d) * r).astype(BF16) for zk in zks]

    per_chunk = GM_VCHUNK // GM_GROUP_DIM
    for gi in range(GM_GROUPS):
        cols = slice(gi * GM_GROUP_DIM, (gi + 1) * GM_GROUP_DIM)
        u = gelu2(_dot(h, win_ref[:, cols]))
        ws = ws_ref[0, gi]
        bias = lnb_ref[0][:, cols] * jnp.sum(ws.astype(F32), axis=1, keepdims=True) + bs_ref[0, gi]
        gain = lng_ref[0][:, cols]
        vcols = slice((gi % per_chunk) * GM_GROUP_DIM, (gi % per_chunk + 1) * GM_GROUP_DIM)
        for c in range(TM // GM_CHUNK):
            rows = slice(c * GM_CHUNK, (c + 1) * GM_CHUNK)
            mixed = _dot(ws, vn[gi // per_chunk][rows, vcols]) * gain + bias
            a_ref[rows, cols] = (u[rows] * mixed).astype(BF16)


def _gmlp(xs, mods, layer, j, g, p, n_tiles):
    row = lambda t: (t, 0)
    return pl.pallas_call(
        functools.partial(_gmlp_kernel, j=j),
        grid=(n_tiles,),
        in_specs=[
            pl.BlockSpec((TM, D), row),
            _mod_spec(layer),
            _layer_spec((1, D), layer),
            pl.BlockSpec(memory_space=pl.ANY),
            _layer_spec((1, GM_HALF), j),
            _layer_spec((1, GM_HALF), j),
            _layer_spec((GM_GROUPS, GM_CHUNK, GM_CHUNK), j),
            _layer_spec((GM_GROUPS, GM_CHUNK, 1), j),
        ],
        out_specs=pl.BlockSpec((TM, GM_HALF), row),
        out_shape=jax.ShapeDtypeStruct((n_tiles * TM, GM_HALF), BF16),
        scratch_shapes=[pltpu.VMEM((D, 2 * GM_HALF), BF16)] + _fill_scratch(),
        compiler_params=_params(1),
        name="gmlp_mix",
    )(xs, mods, g, p["w_in"], p["ln_g"], p["ln_b"], p["ws"], p["bs"])


def _out_ffn_kernel(*refs, n_x, n_a, layer, j):
    x_refs, a_refs = refs[:n_x], refs[n_x:n_x + n_a]
    mod_ref, g_ref, wo_hbm, w1_hbm, w2_hbm, o_ref, wo_ref, w1_ref, w2_ref, stage_ref, sem_ref = refs[n_x + n_a:]
    wo_b = _weight_blocks(wo_hbm, j, wo_ref)
    w1_b = _weight_blocks(w1_hbm, layer, w1_ref)
    w2_b = _weight_blocks(w2_hbm, layer, w2_ref)
    n_chunks = FFN_HIDDEN // FFN_CHUNK
    w1_rows, w2_rows = D // FILL_BLOCK[0], FFN_CHUNK // FILL_BLOCK[0]
    per_chunk = [[w1_b[r * n_chunks + c] for r in range(w1_rows)] + w2_b[c * w2_rows:(c + 1) * w2_rows]
                 for c in range(n_chunks)]
    blocks = wo_b + [b for chunk in per_chunk for b in chunk]

    def body(ensure):
        mod = mod_ref[0]
        ensure(len(wo_b))
        x1 = _load_rows(x_refs) + mod[:, 2 * D:3 * D] * _dot(_load_rows(a_refs), wo_ref[...])
        h = _modulate(x1, g_ref[0], mod[:, 3 * D:4 * D], mod[:, 4 * D:5 * D]).astype(BF16)
        acc = jnp.zeros((TM, D), F32)
        for c in range(n_chunks):
            cols = slice(c * FFN_CHUNK, (c + 1) * FFN_CHUNK)
            ensure(len(wo_b) + c * len(per_chunk[0]) + w1_rows)
            hid = jnp.maximum(_dot(h, w1_ref[:, cols]), 0.0)
            ensure(len(wo_b) + (c + 1) * len(per_chunk[0]))
            acc = acc + _dot((hid * hid).astype(BF16), w2_ref[cols, :])
        o_ref[...] = x1 + mod[:, 5 * D:6 * D] * acc

    first = pl.program_id(0) == 0

    @pl.when(first)
    def _():
        body(_WeightFill(blocks, stage_ref, sem_ref).ensure)

    @pl.when(jnp.logical_not(first))
    def _():
        body(lambda count: None)


def _out_ffn(xs, a, mods, layer, j, wo, g, w1, w2, n_tiles):
    x_specs, x_args = _row_specs(xs)
    a_specs, a_args = _row_specs(a)
    row = lambda t: (t, 0)
    ka = a_args[0].shape[1]
    hbm = pl.BlockSpec(memory_space=pl.ANY)
    return pl.pallas_call(
        functools.partial(_out_ffn_kernel, n_x=len(x_args), n_a=len(a_args), layer=layer, j=j),
        grid=(n_tiles,),
        in_specs=x_specs + a_specs + [_mod_spec(layer), _layer_spec((1, D), layer), hbm, hbm, hbm],
        out_specs=pl.BlockSpec((TM, D), row),
        out_shape=jax.ShapeDtypeStruct((n_tiles * TM, D), F32),
        scratch_shapes=[pltpu.VMEM((ka, D), BF16), pltpu.VMEM((D, FFN_HIDDEN), BF16),
                        pltpu.VMEM((FFN_HIDDEN, D), BF16)] + _fill_scratch(),
        compiler_params=_params(1),
        name="out_ffn",
    )(*x_args, *a_args, mods, g, wo, w1, w2)


def _rot_perm_sign():
    q = QK_ROPE // 4
    perm = np.concatenate([np.arange(q, 2 * q), np.arange(0, q), np.arange(3 * q, 4 * q), np.arange(2 * q, 3 * q)])
    sign = np.concatenate([-np.ones(q), np.ones(q), -np.ones(q), np.ones(q)]).astype(np.float32)
    return perm, sign


def _gain_rows(gain):
    perm, _ = _rot_perm_sign()
    n = gain.shape[0]
    pad = jnp.zeros((n, LANES - QK_ROPE), F32)
    rows = jnp.stack([gain[:, :QK_NOPE],
                      jnp.concatenate([gain[:, QK_NOPE:], pad], axis=1),
                      jnp.concatenate([gain[:, QK_NOPE:][:, perm], pad], axis=1)], axis=1)
    return jnp.concatenate([rows, jnp.zeros((n, 8 - 3, LANES), F32)], axis=1)


def _mla_params(wq_a, q_a_norm, wq_b, wkv_a, kv_a_norm, wkv_b, q_norm, k_norm):
    perm, sign = _rot_perm_sign()
    n = wq_a.shape[0]
    def by_head_pair(first, second):
        r = first.shape[1]
        parts = [t.reshape(n, r, H // 2, 2, LANES) for t in (first, second)]
        return jnp.stack(parts, axis=3).reshape(n, r, 2 * H * LANES)

    wq_b = wq_b.reshape(n, Q_LORA, H, QK_DIM)
    q_rope = wq_b[..., QK_NOPE:]
    wqb = by_head_pair(wq_b[..., :QK_NOPE], jnp.concatenate([q_rope, q_rope[..., perm] * sign], axis=-1))
    k_rope = wkv_a[..., KV_LORA:]
    wkva = jnp.concatenate([wkv_a, k_rope[..., perm] * sign], axis=-1)
    wkv_b = wkv_b.reshape(n, KV_LORA, H, QK_NOPE + V_DIM)
    wkvb = by_head_pair(wkv_b[..., :QK_NOPE], wkv_b[..., QK_NOPE:])
    return dict(wqa=wq_a.astype(BF16), qan=q_a_norm.reshape(n, 1, Q_LORA), wqb=wqb.astype(BF16),
                wkva=wkva.astype(BF16), kvan=kv_a_norm.reshape(n, 1, KV_LORA), wkvb=wkvb.astype(BF16),
                gq=_gain_rows(q_norm), gk=_gain_rows(k_norm))


def _rope_tables():
    row = np.repeat(np.arange(S // GRID_W, dtype=np.float32), GRID_W)
    col = np.tile(np.arange(GRID_W, dtype=np.float32), S // GRID_W)
    half = QK_ROPE // 2
    inv = (np.float32(ROPE_THETA) ** (-np.arange(0, half, 2, dtype=np.float32) / np.float32(half))).astype(np.float32)
    ang_r = row[:, None] * inv[None, :]
    ang_c = col[:, None] * inv[None, :]
    ang = np.concatenate([ang_r, ang_r, ang_c, ang_c], axis=-1)
    cos = np.zeros((S + TM, LANES), np.float32)
    sin = np.zeros((S + TM, LANES), np.float32)
    cos[:S, :QK_ROPE] = np.cos(ang)
    sin[:S, :QK_ROPE] = np.sin(ang)
    cos[S:, :QK_ROPE] = 1.0
    return jnp.asarray(cos), jnp.asarray(sin)


def kernel(x, c, ctx, c_ctx, ada_w, ada_b, norm_mix_g, norm_ffn_g, mla_wq_a, mla_q_a_norm, mla_wq_b,
           mla_wkv_a, mla_kv_a_norm, mla_wkv_b, mla_q_norm, mla_k_norm, mla_wo, gm_w_in, gm_ln_g,
           gm_ln_b, gm_ws, gm_bs, gm_w_out, ffn_w1, ffn_w2):
    c8 = jnp.concatenate([c, c_ctx[None], jnp.zeros((MOD_ROWS - B - 1, D), F32)])
    mods = _adaln(c8, ada_w, ada_b).reshape(DEPTH * MOD_ROWS, 1, N_MOD * D)
    cos_t, sin_t = _rope_tables()

    g_mix = norm_mix_g.reshape(DEPTH, 1, D)
    g_ffn = norm_ffn_g.reshape(DEPTH, 1, D)
    w1, w2 = ffn_w1, ffn_w2
    mla = _mla_params(mla_wq_a, mla_q_a_norm, mla_wq_b, mla_wkv_a, mla_kv_a_norm, mla_wkv_b,
                      mla_q_norm, mla_k_norm)
    gm = dict(w_in=gm_w_in, ln_g=gm_ln_g.reshape(-1, 1, GM_HALF),
              ln_b=gm_ln_b.reshape(-1, 1, GM_HALF), ws=(0.5 * gm_ws).astype(BF16),
              bs=(0.5 * gm_bs).reshape(-1, GM_GROUPS, GM_CHUNK, 1))

    xs = (x.reshape(N_LAT, D), ctx.reshape(N_CTX, D))
    for i in range(DEPTH):
        j = i // 2
        if i % 2 == 0:
            q, k, v = _mla_proj(xs, mods, i, j, g_mix, mla, cos_t, sin_t)
            a = _attn_lat(_plain_softmax_ok(mla_q_norm[j], mla_k_norm[j]), q, k, v)
            ctx_live = i + 2 < DEPTH
            if ctx_live:
                a = (a, _attn_ctx(q, k, v))
            n_tiles = ALL_TILES if ctx_live else LAT_TILES
            xs = _out_ffn(xs, a, mods, i, j, mla_wo, g_ffn, w1, w2, n_tiles)
        else:
            n_tiles = LAT_TILES if i == DEPTH - 1 else ALL_TILES
            a = _gmlp(xs, mods, i, j, g_mix, gm, n_tiles)
            xs = _out_ffn(xs, a, mods, i, j, gm_w_out, g_ffn, w1, w2, n_tiles)
    return xs[:N_LAT].reshape(B, S, D)
```

```python
import functools

import jax
import jax.numpy as jnp
import numpy as np
from jax import lax
from jax.experimental import pallas as pl
from jax.experimental.pallas import tpu as pltpu

F32 = jnp.float32
BF16 = jnp.bfloat16

D = 1024
B = 4
S = 4096
DEPTH = 4
GRID_W = 64
CTX = 256
H = 8
QK_NOPE = 128
QK_ROPE = 64
V_DIM = 128
QK_DIM = QK_NOPE + QK_ROPE
Q_LORA = 384
KV_LORA = 256
ROPE_THETA = 10000.0
GM_CHUNK = 128
GM_GROUPS = 8
GM_HALF = 2 * D
GM_GROUP_DIM = GM_HALF // GM_GROUPS
FFN_HIDDEN = 4 * D
N_MOD = 6
EPS = 1e-6

N_LAT = B * S
N_CTX = B * CTX
N_ALL = N_LAT + N_CTX
LANES = 128
HEAD_PAD = 2 * LANES
TM = 512
TM_MIX = 1024
TQ = 1024
TK = 512
ATTN_HEADS = 2
VT_ROWS = V_DIM + 16
PLAIN_SCORE_LIMIT = 64.0
MOD_ROWS = 8
CTX_MOD_ROW = B
ADA_TN = 1536
FFN_CHUNK = 1024
GM_VCHUNK = 512
FILL_BLOCK = (512, 1024)
VMEM_LIMIT = 56 * 1024 * 1024


def _params(grid_rank):
    return pltpu.CompilerParams(dimension_semantics=("arbitrary",) * grid_rank,
                                vmem_limit_bytes=VMEM_LIMIT)


def _mod_spec(layer, tm):
    def index(t):
        return (layer * MOD_ROWS + jnp.where(t < N_LAT // tm, t // (S // tm), CTX_MOD_ROW), 0, 0)
    return pl.BlockSpec((1, 1, N_MOD * D), index)


def _layer_spec(shape, layer):
    nd = len(shape)
    return pl.BlockSpec((1,) + tuple(shape), lambda *_: (layer,) + (0,) * nd,
                        pipeline_mode=pl.Buffered(1))


def _row_specs(xs, tm):
    if isinstance(xs, tuple):
        width, lat_tiles = xs[0].shape[1], N_LAT // tm
        return [pl.BlockSpec((tm, width), lambda t: (jnp.minimum(t, lat_tiles - 1), 0)),
                pl.BlockSpec((tm, width), lambda t: (jnp.maximum(t - lat_tiles, 0), 0))], list(xs)
    return [pl.BlockSpec((tm, xs.shape[1]), lambda t: (t, 0))], [xs]


def _load_rows(x_refs):
    if len(x_refs) == 1:
        return x_refs[0][...]
    lat_tiles = N_LAT // x_refs[0].shape[0]
    return jnp.where(pl.program_id(0) < lat_tiles, x_refs[0][...], x_refs[1][...])


def _weight_blocks(w_hbm, layer, w_vmem):
    rows, cols = w_vmem.shape
    br, bc = FILL_BLOCK
    return [(w_hbm.at[layer, pl.ds(r, br), pl.ds(c, bc)], w_vmem.at[pl.ds(r, br), pl.ds(c, bc)])
            for r in range(0, rows, br) for c in range(0, cols, bc)]


def _fill_bf16_weights(blocks, stage_ref, sem_ref):
    @pl.when(pl.program_id(0) == 0)
    def _():
        def copy(i):
            return pltpu.make_async_copy(blocks[i][0], stage_ref.at[i % 2], sem_ref.at[i % 2])

        copy(0).start()
        for i in range(len(blocks)):
            if i + 1 < len(blocks):
                copy(i + 1).start()
            copy(i).wait()
            blocks[i][1][...] = stage_ref[i % 2].astype(BF16)


def _fill_scratch():
    return [pltpu.VMEM((2,) + FILL_BLOCK, F32), pltpu.SemaphoreType.DMA((2,))]


def _rms(x):
    return x * lax.rsqrt(jnp.mean(x * x, axis=-1, keepdims=True) + EPS)


def _modulate(x, g, shift, scale):
    return _rms(x) * (g * (1.0 + scale)) + shift


def _dot(a, b):
    return jnp.dot(a, b, preferred_element_type=F32)


def _adaln_kernel(c_ref, w_ref, b_ref, o_ref):
    c = c_ref[...]
    s = c * jax.nn.sigmoid(c)
    o_ref[0] = _dot(s.astype(BF16), w_ref[0].astype(BF16)) + b_ref[0]


def _adaln(c8, ada_w, ada_b):
    return pl.pallas_call(
        _adaln_kernel,
        grid=(DEPTH, N_MOD * D // ADA_TN),
        in_specs=[
            pl.BlockSpec((MOD_ROWS, D), lambda i, j: (0, 0)),
            pl.BlockSpec((1, D, ADA_TN), lambda i, j: (i, 0, j)),
            pl.BlockSpec((1, 1, ADA_TN), lambda i, j: (i, 0, j)),
        ],
        out_specs=pl.BlockSpec((1, MOD_ROWS, ADA_TN), lambda i, j: (i, 0, j)),
        out_shape=jax.ShapeDtypeStruct((DEPTH, MOD_ROWS, N_MOD * D), F32),
        compiler_params=_params(2),
        name="adaln",
    )(c8, ada_w, ada_b.reshape(DEPTH, 1, N_MOD * D))


def _mla_proj_kernel(*refs, n_x):
    x_refs = refs[:n_x]
    (mod_ref, g_ref, wqa_ref, qan_ref, wqb_ref, wkva_ref, kvan_ref, wkvb_ref, gq_ref, gk_ref,
     cos_ref, sin_ref, q_ref, k_ref, v_ref) = refs[n_x:]
    mod = mod_ref[0]
    x = _load_rows(x_refs)
    gq = gq_ref[0] * (QK_DIM ** -0.5 * np.log2(np.e))
    gk = gk_ref[0]
    inv_dim = 1.0 / QK_DIM
    lane_head = lax.broadcasted_iota(jnp.int32, (HEAD_PAD, HEAD_PAD), 0) // LANES
    pair_ones = (lane_head == lax.broadcasted_iota(jnp.int32, (HEAD_PAD, HEAD_PAD), 1) // LANES).astype(BF16)

    h = _modulate(x, g_ref[0], mod[:, 0:D], mod[:, D:2 * D]).astype(BF16)
    kva = _dot(h, wkva_ref[0])
    ckv = (_rms(kva[:, 0:KV_LORA]) * kvan_ref[0]).astype(BF16)
    cq = (_rms(_dot(h, wqa_ref[0])) * qan_ref[0]).astype(BF16)
    cos = cos_ref[...]
    sin = sin_ref[...]
    q_tabs = (cos * gq[1:2], sin * gq[2:3])
    k_tabs = (cos * gk[1:2], sin * gk[2:3])

    def rope(pair, tabs):
        return pair * tabs[0] + pltpu.roll(pair, QK_ROPE, axis=1) * tabs[1]

    kpair = kva[:, KV_LORA:KV_LORA + LANES]
    k_rope = rope(kpair, k_tabs)
    k_pair_sq = 0.5 * kpair * kpair
    pair_cols = 4 * LANES
    for hd in range(0, H, 2):
        kv4 = _dot(ckv, wkvb_ref[0, :, hd // 2 * pair_cols:(hd // 2 + 1) * pair_cols])
        kn = [kv4[:, i * LANES:(i + 1) * LANES] for i in range(2)]
        sq = jnp.concatenate([n * n + k_pair_sq for n in kn], axis=1)
        r = lax.rsqrt(_dot(sq.astype(BF16), pair_ones) * inv_dim + EPS)
        for i in range(2):
            ri = r[:, i * LANES:(i + 1) * LANES]
            k_ref[hd + i, :, 0:LANES] = (kn[i] * ri * gk[0:1]).astype(BF16)
            k_ref[hd + i, :, LANES:HEAD_PAD] = (k_rope * ri).astype(BF16)
            v_ref[hd + i] = kv4[:, (2 + i) * LANES:(3 + i) * LANES].astype(BF16)
    for hd in range(0, H, 2):
        q4 = _dot(cq, wqb_ref[0, :, hd // 2 * pair_cols:(hd // 2 + 1) * pair_cols])
        qn = [q4[:, i * LANES:(i + 1) * LANES] for i in range(2)]
        qpair = [q4[:, (2 + i) * LANES:(3 + i) * LANES] for i in range(2)]
        sq = jnp.concatenate([n * n + 0.5 * pr * pr for n, pr in zip(qn, qpair)], axis=1)
        r = lax.rsqrt(_dot(sq.astype(BF16), pair_ones) * inv_dim + EPS)
        for i in range(2):
            ri = r[:, i * LANES:(i + 1) * LANES]
            q_ref[hd + i, :, 0:LANES] = (qn[i] * ri * gq[0:1]).astype(BF16)
            q_ref[hd + i, :, LANES:HEAD_PAD] = (rope(qpair[i], q_tabs) * ri).astype(BF16)


def _mla_proj(xs, mods, layer, j, g, p, cos_t, sin_t):
    tm = TM_MIX
    x_specs, x_args = _row_specs(xs, tm)
    pos = lambda t: (jnp.where(t < N_LAT // tm, t % (S // tm), S // tm), 0)
    return pl.pallas_call(
        functools.partial(_mla_proj_kernel, n_x=len(x_args)),
        grid=(N_ALL // tm,),
        in_specs=x_specs + [
            _mod_spec(layer, tm),
            _layer_spec((1, D), layer),
            _layer_spec((D, Q_LORA), j),
            _layer_spec((1, Q_LORA), j),
            _layer_spec((Q_LORA, 2 * H * LANES), j),
            _layer_spec((D, KV_LORA + LANES), j),
            _layer_spec((1, KV_LORA), j),
            _layer_spec((KV_LORA, 2 * H * LANES), j),
            _layer_spec((8, LANES), j),
            _layer_spec((8, LANES), j),
            pl.BlockSpec((tm, LANES), pos),
            pl.BlockSpec((tm, LANES), pos),
        ],
        out_specs=[
            pl.BlockSpec((H, tm, HEAD_PAD), lambda t: (0, t, 0)),
            pl.BlockSpec((H, tm, HEAD_PAD), lambda t: (0, t, 0)),
            pl.BlockSpec((H, tm, V_DIM), lambda t: (0, t, 0)),
        ],
        out_shape=[
            jax.ShapeDtypeStruct((H, N_ALL, HEAD_PAD), BF16),
            jax.ShapeDtypeStruct((H, N_ALL, HEAD_PAD), BF16),
            jax.ShapeDtypeStruct((H, N_ALL, V_DIM), BF16),
        ],
        compiler_params=_params(1),
        name="mla_proj",
    )(*x_args, mods, g, p["wqa"], p["qan"], p["wqb"], p["wkva"], p["kvan"], p["wkvb"],
      p["gq"], p["gk"], cos_t, sin_t)


def _ones_column(rows):
    return (lax.broadcasted_iota(jnp.int32, (rows, LANES), 1) == 0).astype(BF16)


def _scores(q, k):
    return lax.dot_general(q, k, (((1,), (1,)), ((), ())), preferred_element_type=F32)


def _softmax_step(s, v, m, acc):
    m_new = jnp.max(s, axis=-1, keepdims=True)
    if m is not None:
        m_new = jnp.maximum(m, m_new)
    p = jnp.exp2(s - m_new).astype(BF16)
    pv = _dot(p, jnp.concatenate([v, _ones_column(v.shape[0])], axis=1))
    if m is None:
        return m_new, pv
    return m_new, jnp.exp2(m - m_new) * acc + pv


def _plain_step(s, v, m, acc):
    del m
    pv = _dot(jnp.exp2(s).astype(BF16), jnp.concatenate([v, _ones_column(v.shape[0])], axis=1))
    return None, pv if acc is None else acc + pv


def _attn_heads(step, q_ref, kl_ref, kc_ref, vl_ref, vc_ref, o_ref):
    state = [(None, None)] * ATTN_HEADS
    for c in range(S // TK):
        rows = slice(c * TK, (c + 1) * TK)
        for i in range(ATTN_HEADS):
            state[i] = step(_scores(q_ref[i], kl_ref[i, rows, :]), vl_ref[i, rows, :], *state[i])
    for i in range(ATTN_HEADS):
        _, acc = step(_scores(q_ref[i], kc_ref[i]), vc_ref[i], *state[i])
        o_ref[:, i * V_DIM:(i + 1) * V_DIM] = (acc[:, 0:V_DIM] / acc[:, V_DIM:V_DIM + 1]).astype(BF16)


def _attn_heads_keys_on_sublanes(q_ref, kl_ref, kc_ref, o_ref, vt_ref):
    acc = [None] * ATTN_HEADS
    chunks = [(kl_ref, slice(c * TK, (c + 1) * TK), CTX + c * TK) for c in range(S // TK)] + [(kc_ref, slice(0, CTX), 0)]
    for k_ref, rows, col0 in chunks:
        for i in range(ATTN_HEADS):
            st = lax.dot_general(k_ref[i, rows, :], q_ref[i], (((1,), (1,)), ((), ())), preferred_element_type=F32)
            pv = _dot(vt_ref[i, :, col0:col0 + rows.stop - rows.start], jnp.exp2(st).astype(BF16))
            acc[i] = pv if acc[i] is None else acc[i] + pv
    for i in range(ATTN_HEADS):
        o_ref[:, i * V_DIM:(i + 1) * V_DIM] = (acc[i][0:V_DIM] / acc[i][V_DIM:V_DIM + 1]).T.astype(BF16)


def _attn_lat_kernel(plain_ref, q_ref, kl_ref, kc_ref, vl_ref, vc_ref, o_ref, vt_ref):
    plain = plain_ref[0] != 0

    @pl.when(jnp.logical_and(plain, pl.program_id(2) == 0))
    def _():
        ones_row = (lax.broadcasted_iota(jnp.int32, (VT_ROWS - V_DIM, CTX + S), 0) == 0).astype(BF16)
        for i in range(ATTN_HEADS):
            vt_ref[i, 0:V_DIM, 0:CTX] = vc_ref[i].T
            vt_ref[i, 0:V_DIM, CTX:CTX + S] = vl_ref[i].T
            vt_ref[i, V_DIM:VT_ROWS, :] = ones_row

    @pl.when(plain)
    def _():
        _attn_heads_keys_on_sublanes(q_ref, kl_ref, kc_ref, o_ref, vt_ref)

    @pl.when(jnp.logical_not(plain))
    def _():
        _attn_heads(_softmax_step, q_ref, kl_ref, kc_ref, vl_ref, vc_ref, o_ref)


def _plain_softmax_ok(q_gain, k_gain):
    bound = QK_DIM * jnp.max(jnp.abs(q_gain)) * jnp.max(jnp.abs(k_gain)) * (QK_DIM ** -0.5 * np.log2(np.e)) * 1.02
    return (bound <= PLAIN_SCORE_LIMIT).astype(jnp.int32).reshape(1)


def _attn_lat(plain_ok, q, k, v):
    ctx_blk = N_LAT // CTX
    return pl.pallas_call(
        _attn_lat_kernel,
        grid=(B, H // ATTN_HEADS, S // TQ),
        in_specs=[
            pl.BlockSpec(memory_space=pltpu.SMEM),
            pl.BlockSpec((ATTN_HEADS, TQ, HEAD_PAD), lambda b, h, i: (h, b * (S // TQ) + i, 0)),
            pl.BlockSpec((ATTN_HEADS, S, HEAD_PAD), lambda b, h, i: (h, b, 0)),
            pl.BlockSpec((ATTN_HEADS, CTX, HEAD_PAD), lambda b, h, i: (h, ctx_blk + b, 0)),
            pl.BlockSpec((ATTN_HEADS, S, V_DIM), lambda b, h, i: (h, b, 0)),
            pl.BlockSpec((ATTN_HEADS, CTX, V_DIM), lambda b, h, i: (h, ctx_blk + b, 0)),
        ],
        out_specs=pl.BlockSpec((TQ, ATTN_HEADS * V_DIM), lambda b, h, i: (b * (S // TQ) + i, h)),
        out_shape=jax.ShapeDtypeStruct((N_LAT, H * V_DIM), BF16),
        scratch_shapes=[pltpu.VMEM((ATTN_HEADS, VT_ROWS, CTX + S), BF16)],
        compiler_params=_params(3),
        name="attn_lat",
    )(plain_ok, q, k, k, v, v)


def _attn_ctx_kernel(q_ref, k_ref, v_ref, o_ref):
    for hd in range(H):
        _, acc = _softmax_step(_scores(q_ref[hd], k_ref[hd]), v_ref[hd], None, None)
        o_ref[:, hd * V_DIM:(hd + 1) * V_DIM] = (acc[:, 0:V_DIM] / acc[:, V_DIM:V_DIM + 1]).astype(BF16)


def _attn_ctx(q, k, v):
    blk = lambda b: (0, N_LAT // CTX + b, 0)
    return pl.pallas_call(
        _attn_ctx_kernel,
        grid=(B,),
        in_specs=[
            pl.BlockSpec((H, CTX, HEAD_PAD), blk),
            pl.BlockSpec((H, CTX, HEAD_PAD), blk),
            pl.BlockSpec((H, CTX, V_DIM), blk),
        ],
        out_specs=pl.BlockSpec((CTX, H * V_DIM), lambda b: (b, 0)),
        out_shape=jax.ShapeDtypeStruct((N_CTX, H * V_DIM), BF16),
        compiler_params=_params(1),
        name="attn_ctx",
    )(q, k, v)


def _gmlp_kernel(x_ref, mod_ref, g_ref, win_hbm, lng_ref, lnb_ref, ws_ref, bs_ref, a_ref,
                 win_ref, stage_ref, sem_ref, *, j):
    _fill_bf16_weights(_weight_blocks(win_hbm, j, win_ref), stage_ref, sem_ref)
    mod = mod_ref[0]
    h = _modulate(x_ref[...], g_ref[0], mod[:, 0:D], mod[:, D:2 * D]).astype(BF16)
    sqrt_half = np.float32(np.sqrt(0.5))

    def gelu2(z):
        return z * (1.0 + lax.erf(z * sqrt_half))

    def lane_partial(a):
        acc = a[:, 0:LANES]
        for i in range(1, a.shape[1] // LANES):
            acc = acc + a[:, i * LANES:(i + 1) * LANES]
        return acc

    zks = []
    for c in range(GM_HALF // GM_VCHUNK):
        z = gelu2(_dot(h, win_ref[:, GM_HALF + c * GM_VCHUNK:GM_HALF + (c + 1) * GM_VCHUNK]))
        if c == 0:
            k = jnp.mean(z, axis=-1, keepdims=True)
        zk = z - k
        zks.append(zk)
        s1 = lane_partial(zk) if c == 0 else s1 + lane_partial(zk)
        s2 = lane_partial(zk * zk) if c == 0 else s2 + lane_partial(zk * zk)
    d = jnp.sum(s1, axis=-1, keepdims=True) * (1.0 / GM_HALF)
    var = jnp.sum(s2, axis=-1, keepdims=True) * (1.0 / GM_HALF) - d * d
    r = lax.rsqrt(var + 4.0 * EPS)
    vn = [((zk - d) * r).astype(BF16) for zk in zks]

    per_chunk = GM_VCHUNK // GM_GROUP_DIM
    for gi in range(GM_GROUPS):
        cols = slice(gi * GM_GROUP_DIM, (gi + 1) * GM_GROUP_DIM)
        u = gelu2(_dot(h, win_ref[:, cols]))
        ws = ws_ref[0, gi]
        bias = lnb_ref[0][:, cols] * jnp.sum(ws.astype(F32), axis=1, keepdims=True) + bs_ref[0, gi]
        gain = lng_ref[0][:, cols]
        vcols = slice((gi % per_chunk) * GM_GROUP_DIM, (gi % per_chunk + 1) * GM_GROUP_DIM)
        for c in range(x_ref.shape[0] // GM_CHUNK):
            rows = slice(c * GM_CHUNK, (c + 1) * GM_CHUNK)
            mixed = _dot(ws, vn[gi // per_chunk][rows, vcols]) * gain + bias
            a_ref[rows, cols] = (u[rows] * mixed).astype(BF16)


def _gmlp(xs, mods, layer, j, g, p, n_rows):
    tm = TM_MIX
    row = lambda t: (t, 0)
    return pl.pallas_call(
        functools.partial(_gmlp_kernel, j=j),
        grid=(n_rows // tm,),
        in_specs=[
            pl.BlockSpec((tm, D), row),
            _mod_spec(layer, tm),
            _layer_spec((1, D), layer),
            pl.BlockSpec(memory_space=pl.ANY),
            _layer_spec((1, GM_HALF), j),
            _layer_spec((1, GM_HALF), j),
            _layer_spec((GM_GROUPS, GM_CHUNK, GM_CHUNK), j),
            _layer_spec((GM_GROUPS, GM_CHUNK, 1), j),
        ],
        out_specs=pl.BlockSpec((tm, GM_HALF), row),
        out_shape=jax.ShapeDtypeStruct((n_rows, GM_HALF), BF16),
        scratch_shapes=[pltpu.VMEM((D, 2 * GM_HALF), BF16)] + _fill_scratch(),
        compiler_params=_params(1),
        name="gmlp_mix",
    )(xs, mods, g, p["w_in"], p["ln_g"], p["ln_b"], p["ws"], p["bs"])


def _out_ffn_kernel(*refs, n_x, n_a, layer, j):
    x_refs, a_refs = refs[:n_x], refs[n_x:n_x + n_a]
    mod_ref, g_ref, wo_hbm, w1_hbm, w2_hbm, o_ref, wo_ref, w1_ref, w2_ref, stage_ref, sem_ref = refs[n_x + n_a:]
    _fill_bf16_weights(_weight_blocks(wo_hbm, j, wo_ref) + _weight_blocks(w1_hbm, layer, w1_ref)
                       + _weight_blocks(w2_hbm, layer, w2_ref), stage_ref, sem_ref)
    mod = mod_ref[0]
    x1 = _load_rows(x_refs) + mod[:, 2 * D:3 * D] * _dot(_load_rows(a_refs), wo_ref[...])
    h = _modulate(x1, g_ref[0], mod[:, 3 * D:4 * D], mod[:, 4 * D:5 * D]).astype(BF16)
    acc = jnp.zeros((TM, D), F32)
    for c in range(FFN_HIDDEN // FFN_CHUNK):
        cols = slice(c * FFN_CHUNK, (c + 1) * FFN_CHUNK)
        hid = jnp.maximum(_dot(h, w1_ref[:, cols]), 0.0)
        acc = acc + _dot((hid * hid).astype(BF16), w2_ref[cols, :])
    o_ref[...] = x1 + mod[:, 5 * D:6 * D] * acc


def _out_ffn(xs, a, mods, layer, j, wo, g, w1, w2, n_rows):
    x_specs, x_args = _row_specs(xs, TM)
    a_specs, a_args = _row_specs(a, TM)
    row = lambda t: (t, 0)
    ka = a_args[0].shape[1]
    hbm = pl.BlockSpec(memory_space=pl.ANY)
    return pl.pallas_call(
        functools.partial(_out_ffn_kernel, n_x=len(x_args), n_a=len(a_args), layer=layer, j=j),
        grid=(n_rows // TM,),
        in_specs=x_specs + a_specs + [_mod_spec(layer, TM), _layer_spec((1, D), layer), hbm, hbm, hbm],
        out_specs=pl.BlockSpec((TM, D), row),
        out_shape=jax.ShapeDtypeStruct((n_rows, D), F32),
        scratch_shapes=[pltpu.VMEM((ka, D), BF16), pltpu.VMEM((D, FFN_HIDDEN), BF16),
                        pltpu.VMEM((FFN_HIDDEN, D), BF16)] + _fill_scratch(),
        compiler_params=_params(1),
        name="out_ffn",
    )(*x_args, *a_args, mods, g, wo, w1, w2)


def _rot_perm_sign():
    q = QK_ROPE // 4
    perm = np.concatenate([np.arange(q, 2 * q), np.arange(0, q), np.arange(3 * q, 4 * q), np.arange(2 * q, 3 * q)])
    sign = np.concatenate([-np.ones(q), np.ones(q), -np.ones(q), np.ones(q)]).astype(np.float32)
    return perm, sign


def _gain_rows(gain):
    perm, _ = _rot_perm_sign()
    n = gain.shape[0]
    pad = jnp.zeros((n, LANES - QK_ROPE), F32)
    rows = jnp.stack([gain[:, :QK_NOPE],
                      jnp.concatenate([gain[:, QK_NOPE:], pad], axis=1),
                      jnp.concatenate([gain[:, QK_NOPE:][:, perm], pad], axis=1)], axis=1)
    return jnp.concatenate([rows, jnp.zeros((n, 8 - 3, LANES), F32)], axis=1)


def _mla_params(wq_a, q_a_norm, wq_b, wkv_a, kv_a_norm, wkv_b, q_norm, k_norm):
    perm, sign = _rot_perm_sign()
    n = wq_a.shape[0]
    def by_head_pair(first, second):
        r = first.shape[1]
        parts = [t.reshape(n, r, H // 2, 2, LANES) for t in (first, second)]
        return jnp.stack(parts, axis=3).reshape(n, r, 2 * H * LANES)

    wq_b = wq_b.reshape(n, Q_LORA, H, QK_DIM)
    q_rope = wq_b[..., QK_NOPE:]
    wqb = by_head_pair(wq_b[..., :QK_NOPE], jnp.concatenate([q_rope, q_rope[..., perm] * sign], axis=-1))
    k_rope = wkv_a[..., KV_LORA:]
    wkva = jnp.concatenate([wkv_a, k_rope[..., perm] * sign], axis=-1)
    wkv_b = wkv_b.reshape(n, KV_LORA, H, QK_NOPE + V_DIM)
    wkvb = by_head_pair(wkv_b[..., :QK_NOPE], wkv_b[..., QK_NOPE:])
    return dict(wqa=wq_a.astype(BF16), qan=q_a_norm.reshape(n, 1, Q_LORA), wqb=wqb.astype(BF16),
                wkva=wkva.astype(BF16), kvan=kv_a_norm.reshape(n, 1, KV_LORA), wkvb=wkvb.astype(BF16),
                gq=_gain_rows(q_norm), gk=_gain_rows(k_norm))


def _rope_tables():
    row = np.repeat(np.arange(S // GRID_W, dtype=np.float32), GRID_W)
    col = np.tile(np.arange(GRID_W, dtype=np.float32), S // GRID_W)
    half = QK_ROPE // 2
    inv = (np.float32(ROPE_THETA) ** (-np.arange(0, half, 2, dtype=np.float32) / np.float32(half))).astype(np.float32)
    ang_r = row[:, None] * inv[None, :]
    ang_c = col[:, None] * inv[None, :]
    ang = np.concatenate([ang_r, ang_r, ang_c, ang_c], axis=-1)
    cos = np.zeros((S + TM_MIX, LANES), np.float32)
    sin = np.zeros((S + TM_MIX, LANES), np.float32)
    cos[:S, :QK_ROPE] = np.cos(ang)
    sin[:S, :QK_ROPE] = np.sin(ang)
    cos[S:, :QK_ROPE] = 1.0
    return jnp.asarray(cos), jnp.asarray(sin)


def kernel(x, c, ctx, c_ctx, ada_w, ada_b, norm_mix_g, norm_ffn_g, mla_wq_a, mla_q_a_norm, mla_wq_b,
           mla_wkv_a, mla_kv_a_norm, mla_wkv_b, mla_q_norm, mla_k_norm, mla_wo, gm_w_in, gm_ln_g,
           gm_ln_b, gm_ws, gm_bs, gm_w_out, ffn_w1, ffn_w2):
    c8 = jnp.concatenate([c, c_ctx[None], jnp.zeros((MOD_ROWS - B - 1, D), F32)])
    mods = _adaln(c8, ada_w, ada_b).reshape(DEPTH * MOD_ROWS, 1, N_MOD * D)
    cos_t, sin_t = _rope_tables()

    g_mix = norm_mix_g.reshape(DEPTH, 1, D)
    g_ffn = norm_ffn_g.reshape(DEPTH, 1, D)
    w1, w2 = ffn_w1, ffn_w2
    mla = _mla_params(mla_wq_a, mla_q_a_norm, mla_wq_b, mla_wkv_a, mla_kv_a_norm, mla_wkv_b,
                      mla_q_norm, mla_k_norm)
    gm = dict(w_in=gm_w_in, ln_g=gm_ln_g.reshape(-1, 1, GM_HALF),
              ln_b=gm_ln_b.reshape(-1, 1, GM_HALF), ws=(0.5 * gm_ws).astype(BF16),
              bs=(0.5 * gm_bs).reshape(-1, GM_GROUPS, GM_CHUNK, 1))

    xs = (x.reshape(N_LAT, D), ctx.reshape(N_CTX, D))
    for i in range(DEPTH):
        j = i // 2
        if i % 2 == 0:
            q, k, v = _mla_proj(xs, mods, i, j, g_mix, mla, cos_t, sin_t)
            a = _attn_lat(_plain_softmax_ok(mla_q_norm[j], mla_k_norm[j]), q, k, v)
            ctx_live = i + 2 < DEPTH
            if ctx_live:
                a = (a, _attn_ctx(q, k, v))
            n_rows = N_ALL if ctx_live else N_LAT
            xs = _out_ffn(xs, a, mods, i, j, mla_wo, g_ffn, w1, w2, n_rows)
        else:
            n_rows = N_LAT if i == DEPTH - 1 else N_ALL
            a = _gmlp(xs, mods, i, j, g_mix, gm, n_rows)
            xs = _out_ffn(xs, a, mods, i, j, gm_w_out, g_ffn, w1, w2, n_rows)
    return xs[:N_LAT].reshape(B, S, D)
```

```python
import functools

import jax
import jax.numpy as jnp
import numpy as np
from jax import lax
from jax.experimental import pallas as pl
from jax.experimental.pallas import tpu as pltpu

F32 = jnp.float32
BF16 = jnp.bfloat16

D = 1024
B = 4
S = 4096
DEPTH = 4
GRID_W = 64
CTX = 256
H = 8
QK_NOPE = 128
QK_ROPE = 64
V_DIM = 128
QK_DIM = QK_NOPE + QK_ROPE
Q_LORA = 384
KV_LORA = 256
ROPE_THETA = 10000.0
GM_CHUNK = 128
GM_GROUPS = 8
GM_HALF = 2 * D
GM_GROUP_DIM = GM_HALF // GM_GROUPS
FFN_HIDDEN = 4 * D
N_MOD = 6
EPS = 1e-6

N_LAT = B * S
N_CTX = B * CTX
N_ALL = N_LAT + N_CTX
LANES = 128
HEAD_PAD = 2 * LANES
TM = 512
TM_MIX = 1024
TQ = 1024
TK = 1024
ATTN_HEADS = 2
VT_ROWS = V_DIM + 16
PLAIN_SCORE_LIMIT = 64.0
MOD_ROWS = 8
CTX_MOD_ROW = B
ADA_TN = 1536
FFN_CHUNK = 1024
GM_VCHUNK = 512
FILL_BLOCK = (512, 1024)
VMEM_LIMIT = 56 * 1024 * 1024


def _params(grid_rank):
    return pltpu.CompilerParams(dimension_semantics=("arbitrary",) * grid_rank,
                                vmem_limit_bytes=VMEM_LIMIT)


def _mod_spec(layer, tm):
    def index(t):
        return (layer * MOD_ROWS + jnp.where(t < N_LAT // tm, t // (S // tm), CTX_MOD_ROW), 0, 0)
    return pl.BlockSpec((1, 1, N_MOD * D), index)


def _layer_spec(shape, layer):
    nd = len(shape)
    return pl.BlockSpec((1,) + tuple(shape), lambda *_: (layer,) + (0,) * nd,
                        pipeline_mode=pl.Buffered(1))


def _row_specs(xs, tm):
    if isinstance(xs, tuple):
        width, lat_tiles = xs[0].shape[1], N_LAT // tm
        return [pl.BlockSpec((tm, width), lambda t: (jnp.minimum(t, lat_tiles - 1), 0)),
                pl.BlockSpec((tm, width), lambda t: (jnp.maximum(t - lat_tiles, 0), 0))], list(xs)
    return [pl.BlockSpec((tm, xs.shape[1]), lambda t: (t, 0))], [xs]


def _load_rows(x_refs):
    if len(x_refs) == 1:
        return x_refs[0][...]
    lat_tiles = N_LAT // x_refs[0].shape[0]
    return jnp.where(pl.program_id(0) < lat_tiles, x_refs[0][...], x_refs[1][...])


def _weight_blocks(w_hbm, layer, w_vmem):
    rows, cols = w_vmem.shape
    br, bc = FILL_BLOCK
    return [(w_hbm.at[layer, pl.ds(r, br), pl.ds(c, bc)], w_vmem.at[pl.ds(r, br), pl.ds(c, bc)])
            for r in range(0, rows, br) for c in range(0, cols, bc)]


def _fill_bf16_weights(blocks, stage_ref, sem_ref):
    @pl.when(pl.program_id(0) == 0)
    def _():
        def copy(i):
            return pltpu.make_async_copy(blocks[i][0], stage_ref.at[i % 2], sem_ref.at[i % 2])

        copy(0).start()
        for i in range(len(blocks)):
            if i + 1 < len(blocks):
                copy(i + 1).start()
            copy(i).wait()
            blocks[i][1][...] = stage_ref[i % 2].astype(BF16)


def _fill_scratch():
    return [pltpu.VMEM((2,) + FILL_BLOCK, F32), pltpu.SemaphoreType.DMA((2,))]


def _rms(x):
    return x * lax.rsqrt(jnp.mean(x * x, axis=-1, keepdims=True) + EPS)


def _modulate(x, g, shift, scale):
    return _rms(x) * (g * (1.0 + scale)) + shift


def _dot(a, b):
    return jnp.dot(a, b, preferred_element_type=F32)


def _adaln_kernel(c_ref, w_ref, b_ref, o_ref):
    c = c_ref[...]
    s = c * jax.nn.sigmoid(c)
    o_ref[0] = _dot(s.astype(BF16), w_ref[0].astype(BF16)) + b_ref[0]


def _adaln(c8, ada_w, ada_b):
    return pl.pallas_call(
        _adaln_kernel,
        grid=(DEPTH, N_MOD * D // ADA_TN),
        in_specs=[
            pl.BlockSpec((MOD_ROWS, D), lambda i, j: (0, 0)),
            pl.BlockSpec((1, D, ADA_TN), lambda i, j: (i, 0, j)),
            pl.BlockSpec((1, 1, ADA_TN), lambda i, j: (i, 0, j)),
        ],
        out_specs=pl.BlockSpec((1, MOD_ROWS, ADA_TN), lambda i, j: (i, 0, j)),
        out_shape=jax.ShapeDtypeStruct((DEPTH, MOD_ROWS, N_MOD * D), F32),
        compiler_params=_params(2),
        name="adaln",
    )(c8, ada_w, ada_b.reshape(DEPTH, 1, N_MOD * D))


def _mla_proj_kernel(*refs, n_x):
    x_refs = refs[:n_x]
    (mod_ref, g_ref, wqa_ref, qan_ref, wqb_ref, wkva_ref, kvan_ref, wkvb_ref, gq_ref, gk_ref,
     cos_ref, sin_ref, q_ref, k_ref, v_ref) = refs[n_x:]
    mod = mod_ref[0]
    x = _load_rows(x_refs)
    gq = gq_ref[0] * (QK_DIM ** -0.5 * np.log2(np.e))
    gk = gk_ref[0]
    inv_dim = 1.0 / QK_DIM
    lane_head = lax.broadcasted_iota(jnp.int32, (HEAD_PAD, HEAD_PAD), 0) // LANES
    pair_ones = (lane_head == lax.broadcasted_iota(jnp.int32, (HEAD_PAD, HEAD_PAD), 1) // LANES).astype(BF16)

    h = _modulate(x, g_ref[0], mod[:, 0:D], mod[:, D:2 * D]).astype(BF16)
    kva = _dot(h, wkva_ref[0])
    ckv = (_rms(kva[:, 0:KV_LORA]) * kvan_ref[0]).astype(BF16)
    cq = (_rms(_dot(h, wqa_ref[0])) * qan_ref[0]).astype(BF16)
    cos = cos_ref[...]
    sin = sin_ref[...]
    q_tabs = (cos * gq[1:2], sin * gq[2:3])
    k_tabs = (cos * gk[1:2], sin * gk[2:3])

    def rope(pair, tabs):
        return pair * tabs[0] + pltpu.roll(pair, QK_ROPE, axis=1) * tabs[1]

    kpair = kva[:, KV_LORA:KV_LORA + LANES]
    k_rope = rope(kpair, k_tabs)
    k_pair_sq = 0.5 * kpair * kpair
    pair_cols = 4 * LANES
    for hd in range(0, H, 2):
        kv4 = _dot(ckv, wkvb_ref[0, :, hd // 2 * pair_cols:(hd // 2 + 1) * pair_cols])
        kn = [kv4[:, i * LANES:(i + 1) * LANES] for i in range(2)]
        sq = jnp.concatenate([n * n + k_pair_sq for n in kn], axis=1)
        r = lax.rsqrt(_dot(sq.astype(BF16), pair_ones) * inv_dim + EPS)
        for i in range(2):
            ri = r[:, i * LANES:(i + 1) * LANES]
            k_ref[hd + i, :, 0:LANES] = (kn[i] * ri * gk[0:1]).astype(BF16)
            k_ref[hd + i, :, LANES:HEAD_PAD] = (k_rope * ri).astype(BF16)
            v_ref[hd + i] = kv4[:, (2 + i) * LANES:(3 + i) * LANES].astype(BF16)
    for hd in range(0, H, 2):
        q4 = _dot(cq, wqb_ref[0, :, hd // 2 * pair_cols:(hd // 2 + 1) * pair_cols])
        qn = [q4[:, i * LANES:(i + 1) * LANES] for i in range(2)]
        qpair = [q4[:, (2 + i) * LANES:(3 + i) * LANES] for i in range(2)]
        sq = jnp.concatenate([n * n + 0.5 * pr * pr for n, pr in zip(qn, qpair)], axis=1)
        r = lax.rsqrt(_dot(sq.astype(BF16), pair_ones) * inv_dim + EPS)
        for i in range(2):
            ri = r[:, i * LANES:(i + 1) * LANES]
            q_ref[hd + i, :, 0:LANES] = (qn[i] * ri * gq[0:1]).astype(BF16)
            q_ref[hd + i, :, LANES:HEAD_PAD] = (rope(qpair[i], q_tabs) * ri).astype(BF16)


def _mla_proj(xs, mods, layer, j, g, p, cos_t, sin_t):
    tm = TM_MIX
    x_specs, x_args = _row_specs(xs, tm)
    pos = lambda t: (jnp.where(t < N_LAT // tm, t % (S // tm), S // tm), 0)
    return pl.pallas_call(
        functools.partial(_mla_proj_kernel, n_x=len(x_args)),
        grid=(N_ALL // tm,),
        in_specs=x_specs + [
            _mod_spec(layer, tm),
            _layer_spec((1, D), layer),
            _layer_spec((D, Q_LORA), j),
            _layer_spec((1, Q_LORA), j),
            _layer_spec((Q_LORA, 2 * H * LANES), j),
            _layer_spec((D, KV_LORA + LANES), j),
            _layer_spec((1, KV_LORA), j),
            _layer_spec((KV_LORA, 2 * H * LANES), j),
            _layer_spec((8, LANES), j),
            _layer_spec((8, LANES), j),
            pl.BlockSpec((tm, LANES), pos),
            pl.BlockSpec((tm, LANES), pos),
        ],
        out_specs=[
            pl.BlockSpec((H, tm, HEAD_PAD), lambda t: (0, t, 0)),
            pl.BlockSpec((H, tm, HEAD_PAD), lambda t: (0, t, 0)),
            pl.BlockSpec((H, tm, V_DIM), lambda t: (0, t, 0)),
        ],
        out_shape=[
            jax.ShapeDtypeStruct((H, N_ALL, HEAD_PAD), BF16),
            jax.ShapeDtypeStruct((H, N_ALL, HEAD_PAD), BF16),
            jax.ShapeDtypeStruct((H, N_ALL, V_DIM), BF16),
        ],
        compiler_params=_params(1),
        name="mla_proj",
    )(*x_args, mods, g, p["wqa"], p["qan"], p["wqb"], p["wkva"], p["kvan"], p["wkvb"],
      p["gq"], p["gk"], cos_t, sin_t)


def _ones_column(rows):
    return (lax.broadcasted_iota(jnp.int32, (rows, LANES), 1) == 0).astype(BF16)


def _scores(q, k):
    return lax.dot_general(q, k, (((1,), (1,)), ((), ())), preferred_element_type=F32)


def _softmax_step(s, v, m, acc):
    m_new = jnp.max(s, axis=-1, keepdims=True)
    if m is not None:
        m_new = jnp.maximum(m, m_new)
    p = jnp.exp2(s - m_new).astype(BF16)
    pv = _dot(p, jnp.concatenate([v, _ones_column(v.shape[0])], axis=1))
    if m is None:
        return m_new, pv
    return m_new, jnp.exp2(m - m_new) * acc + pv


def _plain_step(s, v, m, acc):
    del m
    pv = _dot(jnp.exp2(s).astype(BF16), jnp.concatenate([v, _ones_column(v.shape[0])], axis=1))
    return None, pv if acc is None else acc + pv


def _attn_heads(step, q_ref, kl_ref, kc_ref, vl_ref, vc_ref, o_ref):
    state = [(None, None)] * ATTN_HEADS
    for c in range(S // TK):
        rows = slice(c * TK, (c + 1) * TK)
        for i in range(ATTN_HEADS):
            state[i] = step(_scores(q_ref[i], kl_ref[i, rows, :]), vl_ref[i, rows, :], *state[i])
    for i in range(ATTN_HEADS):
        _, acc = step(_scores(q_ref[i], kc_ref[i]), vc_ref[i], *state[i])
        o_ref[:, i * V_DIM:(i + 1) * V_DIM] = (acc[:, 0:V_DIM] / acc[:, V_DIM:V_DIM + 1]).astype(BF16)


def _attn_heads_keys_on_sublanes(q_ref, kl_ref, kc_ref, o_ref, vt_ref):
    acc = [None] * ATTN_HEADS
    chunks = [(kl_ref, slice(c * TK, (c + 1) * TK), CTX + c * TK) for c in range(S // TK)] + [(kc_ref, slice(0, CTX), 0)]
    for k_ref, rows, col0 in chunks:
        for i in range(ATTN_HEADS):
            st = lax.dot_general(k_ref[i, rows, :], q_ref[i], (((1,), (1,)), ((), ())), preferred_element_type=F32)
            pv = _dot(vt_ref[i, :, col0:col0 + rows.stop - rows.start], jnp.exp2(st).astype(BF16))
            acc[i] = pv if acc[i] is None else acc[i] + pv
    for i in range(ATTN_HEADS):
        o_ref[:, i * V_DIM:(i + 1) * V_DIM] = (acc[i][0:V_DIM] / acc[i][V_DIM:V_DIM + 1]).T.astype(BF16)


def _attn_lat_kernel(plain_ref, q_ref, kl_ref, kc_ref, vl_ref, vc_ref, o_ref, vt_ref):
    plain = plain_ref[0] != 0

    @pl.when(jnp.logical_and(plain, pl.program_id(2) == 0))
    def _():
        ones_row = (lax.broadcasted_iota(jnp.int32, (VT_ROWS - V_DIM, CTX + S), 0) == 0).astype(BF16)
        for i in range(ATTN_HEADS):
            vt_ref[i, 0:V_DIM, 0:CTX] = vc_ref[i].T
            vt_ref[i, 0:V_DIM, CTX:CTX + S] = vl_ref[i].T
            vt_ref[i, V_DIM:VT_ROWS, :] = ones_row

    @pl.when(plain)
    def _():
        _attn_heads_keys_on_sublanes(q_ref, kl_ref, kc_ref, o_ref, vt_ref)

    @pl.when(jnp.logical_not(plain))
    def _():
        _attn_heads(_softmax_step, q_ref, kl_ref, kc_ref, vl_ref, vc_ref, o_ref)


def _plain_softmax_ok(q_gain, k_gain):
    bound = QK_DIM * jnp.max(jnp.abs(q_gain)) * jnp.max(jnp.abs(k_gain)) * (QK_DIM ** -0.5 * np.log2(np.e)) * 1.02
    return (bound <= PLAIN_SCORE_LIMIT).astype(jnp.int32).reshape(1)


def _attn_lat(plain_ok, q, k, v):
    ctx_blk = N_LAT // CTX
    return pl.pallas_call(
        _attn_lat_kernel,
        grid=(B, H // ATTN_HEADS, S // TQ),
        in_specs=[
            pl.BlockSpec(memory_space=pltpu.SMEM),
            pl.BlockSpec((ATTN_HEADS, TQ, HEAD_PAD), lambda b, h, i: (h, b * (S // TQ) + i, 0)),
            pl.BlockSpec((ATTN_HEADS, S, HEAD_PAD), lambda b, h, i: (h, b, 0)),
            pl.BlockSpec((ATTN_HEADS, CTX, HEAD_PAD), lambda b, h, i: (h, ctx_blk + b, 0)),
            pl.BlockSpec((ATTN_HEADS, S, V_DIM), lambda b, h, i: (h, b, 0)),
            pl.BlockSpec((ATTN_HEADS, CTX, V_DIM), lambda b, h, i: (h, ctx_blk + b, 0)),
        ],
        out_specs=pl.BlockSpec((TQ, ATTN_HEADS * V_DIM), lambda b, h, i: (b * (S // TQ) + i, h)),
        out_shape=jax.ShapeDtypeStruct((N_LAT, H * V_DIM), BF16),
        scratch_shapes=[pltpu.VMEM((ATTN_HEADS, VT_ROWS, CTX + S), BF16)],
        compiler_params=_params(3),
        name="attn_lat",
    )(plain_ok, q, k, k, v, v)


def _attn_ctx_kernel(q_ref, k_ref, v_ref, o_ref):
    for hd in range(H):
        _, acc = _softmax_step(_scores(q_ref[hd], k_ref[hd]), v_ref[hd], None, None)
        o_ref[:, hd * V_DIM:(hd + 1) * V_DIM] = (acc[:, 0:V_DIM] / acc[:, V_DIM:V_DIM + 1]).astype(BF16)


def _attn_ctx(q, k, v):
    blk = lambda b: (0, N_LAT // CTX + b, 0)
    return pl.pallas_call(
        _attn_ctx_kernel,
        grid=(B,),
        in_specs=[
            pl.BlockSpec((H, CTX, HEAD_PAD), blk),
            pl.BlockSpec((H, CTX, HEAD_PAD), blk),
            pl.BlockSpec((H, CTX, V_DIM), blk),
        ],
        out_specs=pl.BlockSpec((CTX, H * V_DIM), lambda b: (b, 0)),
        out_shape=jax.ShapeDtypeStruct((N_CTX, H * V_DIM), BF16),
        compiler_params=_params(1),
        name="attn_ctx",
    )(q, k, v)


def _gmlp_kernel(x_ref, mod_ref, g_ref, win_hbm, lng_ref, lnb_ref, ws_ref, bs_ref, a_ref,
                 win_ref, stage_ref, sem_ref, *, j):
    _fill_bf16_weights(_weight_blocks(win_hbm, j, win_ref), stage_ref, sem_ref)
    mod = mod_ref[0]
    h = _modulate(x_ref[...], g_ref[0], mod[:, 0:D], mod[:, D:2 * D]).astype(BF16)
    sqrt_half = np.float32(np.sqrt(0.5))

    def gelu2(z):
        return z * (1.0 + lax.erf(z * sqrt_half))

    def lane_partial(a):
        acc = a[:, 0:LANES]
        for i in range(1, a.shape[1] // LANES):
            acc = acc + a[:, i * LANES:(i + 1) * LANES]
        return acc

    zks = []
    for c in range(GM_HALF // GM_VCHUNK):
        z = gelu2(_dot(h, win_ref[:, GM_HALF + c * GM_VCHUNK:GM_HALF + (c + 1) * GM_VCHUNK]))
        if c == 0:
            k = jnp.mean(z, axis=-1, keepdims=True)
        zk = z - k
        zks.append(zk)
        s1 = lane_partial(zk) if c == 0 else s1 + lane_partial(zk)
        s2 = lane_partial(zk * zk) if c == 0 else s2 + lane_partial(zk * zk)
    d = jnp.sum(s1, axis=-1, keepdims=True) * (1.0 / GM_HALF)
    var = jnp.sum(s2, axis=-1, keepdims=True) * (1.0 / GM_HALF) - d * d
    r = lax.rsqrt(var + 4.0 * EPS)
    vn = [((zk - d) * r).astype(BF16) for zk in zks]

    per_chunk = GM_VCHUNK // GM_GROUP_DIM
    for gi in range(GM_GROUPS):
        cols = slice(gi * GM_GROUP_DIM, (gi + 1) * GM_GROUP_DIM)
        u = gelu2(_dot(h, win_ref[:, cols]))
        ws = ws_ref[0, gi]
        bias = lnb_ref[0][:, cols] * jnp.sum(ws.astype(F32), axis=1, keepdims=True) + bs_ref[0, gi]
        gain = lng_ref[0][:, cols]
        vcols = slice((gi % per_chunk) * GM_GROUP_DIM, (gi % per_chunk + 1) * GM_GROUP_DIM)
        for c in range(x_ref.shape[0] // GM_CHUNK):
            rows = slice(c * GM_CHUNK, (c + 1) * GM_CHUNK)
            mixed = _dot(ws, vn[gi // per_chunk][rows, vcols]) * gain + bias
            a_ref[rows, cols] = (u[rows] * mixed).astype(BF16)


def _gmlp(xs, mods, layer, j, g, p, n_rows):
    tm = TM_MIX
    row = lambda t: (t, 0)
    return pl.pallas_call(
        functools.partial(_gmlp_kernel, j=j),
        grid=(n_rows // tm,),
        in_specs=[
            pl.BlockSpec((tm, D), row),
            _mod_spec(layer, tm),
            _layer_spec((1, D), layer),
            pl.BlockSpec(memory_space=pl.ANY),
            _layer_spec((1, GM_HALF), j),
            _layer_spec((1, GM_HALF), j),
            _layer_spec((GM_GROUPS, GM_CHUNK, GM_CHUNK), j),
            _layer_spec((GM_GROUPS, GM_CHUNK, 1), j),
        ],
        out_specs=pl.BlockSpec((tm, GM_HALF), row),
        out_shape=jax.ShapeDtypeStruct((n_rows, GM_HALF), BF16),
        scratch_shapes=[pltpu.VMEM((D, 2 * GM_HALF), BF16)] + _fill_scratch(),
        compiler_params=_params(1),
        name="gmlp_mix",
    )(xs, mods, g, p["w_in"], p["ln_g"], p["ln_b"], p["ws"], p["bs"])


def _out_ffn_kernel(*refs, n_x, n_a, layer, j):
    x_refs, a_refs = refs[:n_x], refs[n_x:n_x + n_a]
    mod_ref, g_ref, wo_hbm, w1_hbm, w2_hbm, o_ref, wo_ref, w1_ref, w2_ref, stage_ref, sem_ref = refs[n_x + n_a:]
    _fill_bf16_weights(_weight_blocks(wo_hbm, j, wo_ref) + _weight_blocks(w1_hbm, layer, w1_ref)
                       + _weight_blocks(w2_hbm, layer, w2_ref), stage_ref, sem_ref)
    mod = mod_ref[0]
    x1 = _load_rows(x_refs) + mod[:, 2 * D:3 * D] * _dot(_load_rows(a_refs), wo_ref[...])
    h = _modulate(x1, g_ref[0], mod[:, 3 * D:4 * D], mod[:, 4 * D:5 * D]).astype(BF16)
    acc = jnp.zeros((TM, D), F32)
    for c in range(FFN_HIDDEN // FFN_CHUNK):
        cols = slice(c * FFN_CHUNK, (c + 1) * FFN_CHUNK)
        hid = jnp.maximum(_dot(h, w1_ref[:, cols]), 0.0)
        acc = acc + _dot((hid * hid).astype(BF16), w2_ref[cols, :])
    o_ref[...] = x1 + mod[:, 5 * D:6 * D] * acc


def _out_ffn(xs, a, mods, layer, j, wo, g, w1, w2, n_rows):
    x_specs, x_args = _row_specs(xs, TM)
    a_specs, a_args = _row_specs(a, TM)
    row = lambda t: (t, 0)
    ka = a_args[0].shape[1]
    hbm = pl.BlockSpec(memory_space=pl.ANY)
    return pl.pallas_call(
        functools.partial(_out_ffn_kernel, n_x=len(x_args), n_a=len(a_args), layer=layer, j=j),
        grid=(n_rows // TM,),
        in_specs=x_specs + a_specs + [_mod_spec(layer, TM), _layer_spec((1, D), layer), hbm, hbm, hbm],
        out_specs=pl.BlockSpec((TM, D), row),
        out_shape=jax.ShapeDtypeStruct((n_rows, D), F32),
        scratch_shapes=[pltpu.VMEM((ka, D), BF16), pltpu.VMEM((D, FFN_HIDDEN), BF16),
                        pltpu.VMEM((FFN_HIDDEN, D), BF16)] + _fill_scratch(),
        compiler_params=_params(1),
        name="out_ffn",
    )(*x_args, *a_args, mods, g, wo, w1, w2)


def _rot_perm_sign():
    q = QK_ROPE // 4
    perm = np.concatenate([np.arange(q, 2 * q), np.arange(0, q), np.arange(3 * q, 4 * q), np.arange(2 * q, 3 * q)])
    sign = np.concatenate([-np.ones(q), np.ones(q), -np.ones(q), np.ones(q)]).astype(np.float32)
    return perm, sign


def _gain_rows(gain):
    perm, _ = _rot_perm_sign()
    n = gain.shape[0]
    pad = jnp.zeros((n, LANES - QK_ROPE), F32)
    rows = jnp.stack([gain[:, :QK_NOPE],
                      jnp.concatenate([gain[:, QK_NOPE:], pad], axis=1),
                      jnp.concatenate([gain[:, QK_NOPE:][:, perm], pad], axis=1)], axis=1)
    return jnp.concatenate([rows, jnp.zeros((n, 8 - 3, LANES), F32)], axis=1)


def _mla_params(wq_a, q_a_norm, wq_b, wkv_a, kv_a_norm, wkv_b, q_norm, k_norm):
    perm, sign = _rot_perm_sign()
    n = wq_a.shape[0]
    def by_head_pair(first, second):
        r = first.shape[1]
        parts = [t.reshape(n, r, H // 2, 2, LANES) for t in (first, second)]
        return jnp.stack(parts, axis=3).reshape(n, r, 2 * H * LANES)

    wq_b = wq_b.reshape(n, Q_LORA, H, QK_DIM)
    q_rope = wq_b[..., QK_NOPE:]
    wqb = by_head_pair(wq_b[..., :QK_NOPE], jnp.concatenate([q_rope, q_rope[..., perm] * sign], axis=-1))
    k_rope = wkv_a[..., KV_LORA:]
    wkva = jnp.concatenate([wkv_a, k_rope[..., perm] * sign], axis=-1)
    wkv_b = wkv_b.reshape(n, KV_LORA, H, QK_NOPE + V_DIM)
    wkvb = by_head_pair(wkv_b[..., :QK_NOPE], wkv_b[..., QK_NOPE:])
    return dict(wqa=wq_a.astype(BF16), qan=q_a_norm.reshape(n, 1, Q_LORA), wqb=wqb.astype(BF16),
                wkva=wkva.astype(BF16), kvan=kv_a_norm.reshape(n, 1, KV_LORA), wkvb=wkvb.astype(BF16),
                gq=_gain_rows(q_norm), gk=_gain_rows(k_norm))


def _rope_tables():
    row = np.repeat(np.arange(S // GRID_W, dtype=np.float32), GRID_W)
    col = np.tile(np.arange(GRID_W, dtype=np.float32), S // GRID_W)
    half = QK_ROPE // 2
    inv = (np.float32(ROPE_THETA) ** (-np.arange(0, half, 2, dtype=np.float32) / np.float32(half))).astype(np.float32)
    ang_r = row[:, None] * inv[None, :]
    ang_c = col[:, None] * inv[None, :]
    ang = np.concatenate([ang_r, ang_r, ang_c, ang_c], axis=-1)
    cos = np.zeros((S + TM_MIX, LANES), np.float32)
    sin = np.zeros((S + TM_MIX, LANES), np.float32)
    cos[:S, :QK_ROPE] = np.cos(ang)
    sin[:S, :QK_ROPE] = np.sin(ang)
    cos[S:, :QK_ROPE] = 1.0
    return jnp.asarray(cos), jnp.asarray(sin)


def kernel(x, c, ctx, c_ctx, ada_w, ada_b, norm_mix_g, norm_ffn_g, mla_wq_a, mla_q_a_norm, mla_wq_b,
           mla_wkv_a, mla_kv_a_norm, mla_wkv_b, mla_q_norm, mla_k_norm, mla_wo, gm_w_in, gm_ln_g,
           gm_ln_b, gm_ws, gm_bs, gm_w_out, ffn_w1, ffn_w2):
    c8 = jnp.concatenate([c, c_ctx[None], jnp.zeros((MOD_ROWS - B - 1, D), F32)])
    mods = _adaln(c8, ada_w, ada_b).reshape(DEPTH * MOD_ROWS, 1, N_MOD * D)
    cos_t, sin_t = _rope_tables()

    g_mix = norm_mix_g.reshape(DEPTH, 1, D)
    g_ffn = norm_ffn_g.reshape(DEPTH, 1, D)
    w1, w2 = ffn_w1, ffn_w2
    mla = _mla_params(mla_wq_a, mla_q_a_norm, mla_wq_b, mla_wkv_a, mla_kv_a_norm, mla_wkv_b,
                      mla_q_norm, mla_k_norm)
    gm = dict(w_in=gm_w_in, ln_g=gm_ln_g.reshape(-1, 1, GM_HALF),
              ln_b=gm_ln_b.reshape(-1, 1, GM_HALF), ws=(0.5 * gm_ws).astype(BF16),
              bs=(0.5 * gm_bs).reshape(-1, GM_GROUPS, GM_CHUNK, 1))

    xs = (x.reshape(N_LAT, D), ctx.reshape(N_CTX, D))
    for i in range(DEPTH):
        j = i // 2
        if i % 2 == 0:
            q, k, v = _mla_proj(xs, mods, i, j, g_mix, mla, cos_t, sin_t)
            a = _attn_lat(_plain_softmax_ok(mla_q_norm[j], mla_k_norm[j]), q, k, v)
            ctx_live = i + 2 < DEPTH
            if ctx_live:
                a = (a, _attn_ctx(q, k, v))
            n_rows = N_ALL if ctx_live else N_LAT
            xs = _out_ffn(xs, a, mods, i, j, mla_wo, g_ffn, w1, w2, n_rows)
        else:
            n_rows = N_LAT if i == DEPTH - 1 else N_ALL
            a = _gmlp(xs, mods, i, j, g_mix, gm, n_rows)
            xs = _out_ffn(xs, a, mods, i, j, gm_w_out, g_ffn, w1, w2, n_rows)
    return xs[:N_LAT].reshape(B, S, D)
```
